```python
import math
import numpy as np
import jax
import jax.numpy as jnp
from jax import lax

D_MODEL = 1024
BATCH = 32
SEQ = 256
DEPTH = 2
DEC_BATCH = 4
DEC_SEQ = 4096
PAST_LEN = 256

GRID_W = 64
BLOCK = 128
A_HEADS = 4
A_DQK = 32
A_DV = 64
B_HEADS = 8
B_KV_HEADS = 2
B_GROUP = B_HEADS // B_KV_HEADS
B_DH = 64
WINDOW = 128
C_HEADS = 4
C_Q_RANK = 256
C_KV_RANK = 128
C_NOPE = 64
C_ROPE = 32
C_DV = 64
MIX_WIDTH = A_HEADS * A_DV + B_HEADS * B_DH + C_HEADS * C_DV
IN_WIDTHS = (A_HEADS * 2 * A_DQK, A_HEADS * 2 * A_DQK, A_HEADS * A_DV,
             B_HEADS * B_DH, B_KV_HEADS * B_DH, B_KV_HEADS * B_DH,
             C_Q_RANK, C_KV_RANK, C_ROPE)
IN_WIDTH = sum(IN_WIDTHS)
D_FF = 2816
N_MOD = 9
ROPE_BASE = 10000.0
LN_EPS = 1e-5
RMS_EPS = 1e-6
DN_ALPHA = (2 * DEPTH) ** 0.25
DN_BETA = (8 * DEPTH) ** -0.25
FFN_RES = 0.5

kernel_name = 'hybrid_diff_prefix_trunk_step'


def _layernorm(x, g, b):
    xf = x.astype(jnp.float32)
    mu = jnp.mean(xf, axis=-1, keepdims=True)
    var = jnp.mean(jnp.square(xf - mu), axis=-1, keepdims=True)
    y = (xf - mu) * lax.rsqrt(var + LN_EPS)
    return (y * g.astype(jnp.float32) + b.astype(jnp.float32)).astype(x.dtype)


def _rmsnorm(x, g):
    xf = x.astype(jnp.float32)
    y = xf * lax.rsqrt(jnp.mean(jnp.square(xf), axis=-1, keepdims=True) + RMS_EPS)
    return (y * g.astype(jnp.float32)).astype(x.dtype)


def _modulation(cond, w, b):
    return (jax.nn.silu(cond) @ w + b).reshape(cond.shape[0], N_MOD, D_MODEL)


def _modulate(x, shift, scale):
    return x * (1.0 + scale[:, None, :]) + shift[:, None, :]


def _residual_post_norm(x, f, gate, weight, g, b):
    return _layernorm(DN_ALPHA * x + weight * gate[:, None, :] * f, g, b)


def _ffn_half(x, mod, slot, w1, w3, w2, g, b):
    h = _modulate(x, mod[:, 3 * slot], mod[:, 3 * slot + 1])
    f = (jax.nn.silu(h @ w1) * (h @ w3)) @ w2
    return _residual_post_norm(x, f, mod[:, 3 * slot + 2], FFN_RES, g, b)


def _axial_rope_tables(rows, dim):
    row = jnp.repeat(jnp.arange(rows, dtype=jnp.float32), GRID_W)
    col = jnp.tile(jnp.arange(GRID_W, dtype=jnp.float32), rows)
    a = dim // 2
    inv = jnp.power(ROPE_BASE, -jnp.arange(0, a, 2, dtype=jnp.float32) / a)
    ar = row[:, None] * inv[None, :]
    ac = col[:, None] * inv[None, :]
    ang = jnp.concatenate([ar, ar, ac, ac], axis=-1)
    return jnp.cos(ang), jnp.sin(ang)


def _rope(x, cos, sin):
    dim = x.shape[-1]
    a = dim // 2
    q = a // 2
    xr, xc = x[..., :a], x[..., a:]
    rot = jnp.concatenate([-xr[..., q:], xr[..., :q], -xc[..., q:], xc[..., :q]], axis=-1)
    return x * cos[:, None, :].astype(x.dtype) + rot * sin[:, None, :].astype(x.dtype)


def _sweep_query_blocks(fn, *qs):
    b, n = qs[0].shape[:2]
    nb = n // BLOCK
    blocks = tuple(jnp.swapaxes(q.reshape(b, nb, BLOCK, *q.shape[2:]), 0, 1) for q in qs)
    out = lax.map(lambda qb: fn(*qb), blocks)
    out = jnp.swapaxes(out, 0, 1)
    return out.reshape(b, n, *out.shape[3:])


def _sink_softmax(s, sink):
    m = jnp.maximum(jnp.max(s, axis=-1, keepdims=True), sink)
    e = jnp.exp(s - m)
    return e / (jnp.sum(e, axis=-1, keepdims=True) + jnp.exp(sink - m))


def _diff_lambda(lam_params, lam_init):
    lp = lam_params.astype(jnp.float32)
    return jnp.exp(jnp.sum(lp[0] * lp[1])) - jnp.exp(jnp.sum(lp[2] * lp[3])) + lam_init


def _diff_attn(q, k, v, lam, subln_g, lam_init):
    scale = A_DQK ** -0.5

    def one_block(qb):
        s = jnp.einsum('bqhjd,bkhjd->bhjqk', qb, k).astype(jnp.float32) * scale
        p = jax.nn.softmax(s, axis=-1)
        w = p[:, :, 0] - lam * p[:, :, 1]
        return jnp.einsum('bhqk,bkhd->bqhd', w.astype(v.dtype), v)

    o = _sweep_query_blocks(one_block, q)
    return _rmsnorm(o, subln_g) * (1.0 - lam_init)


def _sink_attn_dense(q, k, v, sink):
    scale = B_DH ** -0.5
    sk = sink.astype(jnp.float32).reshape(1, B_KV_HEADS, B_GROUP, 1, 1)

    def one_block(qb):
        s = jnp.einsum('bqhgd,bkhd->bhgqk', qb, k).astype(jnp.float32) * scale
        p = _sink_softmax(s, sk)
        return jnp.einsum('bhgqk,bkhd->bqhgd', p.astype(v.dtype), v)

    return _sweep_query_blocks(one_block, q)


def _band_blocks(t):
    b, n = t.shape[:2]
    nb = n // BLOCK
    tb = t.reshape(b, nb, BLOCK, *t.shape[2:])
    tp = jnp.pad(tb, ((0, 0), (1, 1)) + ((0, 0),) * (tb.ndim - 2))
    return jnp.concatenate([tp[:, :-2], tp[:, 1:-1], tp[:, 2:]], axis=2)


def _sink_attn_banded(q, k, v, k_ctx, v_ctx, sink):
    b, n = q.shape[:2]
    nb = n // BLOCK
    n_ctx = k_ctx.shape[1]
    scale = B_DH ** -0.5
    qb = q.reshape(b, nb, BLOCK, B_KV_HEADS, B_GROUP, B_DH)
    kw = _band_blocks(k)
    vw = _band_blocks(v)
    s_loc = jnp.einsum('bnqhgd,bnkhd->bnhgqk', qb, kw).astype(jnp.float32) * scale
    s_ctx = jnp.einsum('bnqhgd,bkhd->bnhgqk', qb, k_ctx).astype(jnp.float32) * scale
    blk = jnp.arange(nb)[:, None, None]
    qi = jnp.arange(BLOCK)[None, :, None]
    kj = jnp.arange(3 * BLOCK)[None, None, :]
    kpos = (blk - 1) * BLOCK + kj
    qpos = blk * BLOCK + qi
    valid = (jnp.abs(kpos - qpos) <= WINDOW) & (kpos >= 0) & (kpos < n)
    s_loc = jnp.where(valid[None, :, None, None], s_loc, -jnp.inf)
    sk = sink.astype(jnp.float32).reshape(1, 1, B_KV_HEADS, B_GROUP, 1, 1)
    p = _sink_softmax(jnp.concatenate([s_ctx, s_loc], axis=-1), sk)
    p_ctx = p[..., :n_ctx].astype(v.dtype)
    p_loc = p[..., n_ctx:].astype(v.dtype)
    o = (jnp.einsum('bnhgqk,bkhd->bnqhgd', p_ctx, v_ctx)
         + jnp.einsum('bnhgqk,bnkhd->bnqhgd', p_loc, vw))
    return o.reshape(b, n, B_KV_HEADS, B_GROUP, B_DH)


def _mla(q_nope, q_pe, c_kv, k_pe, w_kv_up):
    w_uk = w_kv_up[..., :C_NOPE]
    w_uv = w_kv_up[..., C_NOPE:]
    q_lat = jnp.einsum('bqhd,chd->bqhc', q_nope, w_uk)
    scale = (C_NOPE + C_ROPE) ** -0.5

    def one_block(ql, qp):
        s = (jnp.einsum('bqhc,bkc->bhqk', ql, c_kv)
             + jnp.einsum('bqhr,bkr->bhqk', qp, k_pe)).astype(jnp.float32) * scale
        p = jax.nn.softmax(s, axis=-1)
        return jnp.einsum('bhqk,bkc->bqhc', p.astype(c_kv.dtype), c_kv)

    o_lat = _sweep_query_blocks(one_block, q_lat, q_pe)
    return jnp.einsum('bqhc,chd->bqhd', o_lat, w_uv)


def _project(h, w_in, q_norm_g, w_q_up, kv_norm_g, ropes):
    b, n, _ = h.shape
    offs = np.cumsum(IN_WIDTHS)[:-1].tolist()
    a_q, a_k, a_v, b_q, b_k, b_v, c_qd, c_kvd, c_kpe = jnp.split(h @ w_in, offs, axis=-1)
    a_q = a_q.reshape(b, n, 2 * A_HEADS, A_DQK)
    a_k = a_k.reshape(b, n, 2 * A_HEADS, A_DQK)
    a_v = a_v.reshape(b, n, A_HEADS, A_DV)
    b_q = b_q.reshape(b, n, B_HEADS, B_DH)
    b_k = b_k.reshape(b, n, B_KV_HEADS, B_DH)
    b_v = b_v.reshape(b, n, B_KV_HEADS, B_DH)
    c_q = (_rmsnorm(c_qd, q_norm_g) @ w_q_up).reshape(b, n, C_HEADS, C_NOPE + C_ROPE)
    c_qn, c_qp = c_q[..., :C_NOPE], c_q[..., C_NOPE:]
    c_kv = _rmsnorm(c_kvd, kv_norm_g)
    c_kpe = c_kpe[:, :, None, :]
    if ropes is not None:
        (ca, sa), (cb, sb), (cc, sc) = ropes
        a_q = _rope(a_q, ca, sa)
        a_k = _rope(a_k, ca, sa)
        b_q = _rope(b_q, cb, sb)
        b_k = _rope(b_k, cb, sb)
        c_qp = _rope(c_qp, cc, sc)
        c_kpe = _rope(c_kpe, cc, sc)
    return (a_q.reshape(b, n, A_HEADS, 2, A_DQK), a_k.reshape(b, n, A_HEADS, 2, A_DQK), a_v,
            b_q.reshape(b, n, B_KV_HEADS, B_GROUP, B_DH), b_k, b_v,
            c_qn, c_qp, c_kv, c_kpe[:, :, 0])


def _context_state(t):
    a_q, a_k, a_v, b_q, b_k, b_v, c_qn, c_qp, c_kv, c_kpe = t
    b, n = a_k.shape[:2]
    return (a_k.reshape(b, n, A_HEADS, 2 * A_DQK), a_v, b_k, b_v, c_kv, c_kpe)


def _mix(t, ctx, lam, lam_init, subln_g, sink, w_kv_up, w_o):
    a_q, a_k, a_v, b_q, b_k, b_v, c_qn, c_qp, c_kv, c_kpe = t
    b, n = a_q.shape[:2]
    if ctx is None:
        o_a = _diff_attn(a_q, a_k, a_v, lam, subln_g, lam_init)
        o_b = _sink_attn_dense(b_q, b_k, b_v, sink)
        o_c = _mla(c_qn, c_qp, c_kv, c_kpe, w_kv_up)
    else:
        ctx_ak, ctx_av, ctx_bk, ctx_bv, ctx_ckv, ctx_ckpe = ctx
        n_ctx = ctx_ak.shape[1]
        ak_all = jnp.concatenate([ctx_ak.reshape(b, n_ctx, A_HEADS, 2, A_DQK), a_k], axis=1)
        av_all = jnp.concatenate([ctx_av, a_v], axis=1)
        o_a = _diff_attn(a_q, ak_all, av_all, lam, subln_g, lam_init)
        o_b = _sink_attn_banded(b_q, b_k, b_v, ctx_bk, ctx_bv, sink)
        o_c = _mla(c_qn, c_qp, jnp.concatenate([ctx_ckv, c_kv], axis=1),
                   jnp.concatenate([ctx_ckpe, c_kpe], axis=1), w_kv_up)
    o = jnp.concatenate([o_a.reshape(b, n, -1), o_b.reshape(b, n, -1), o_c.reshape(b, n, -1)], axis=-1)
    return o @ w_o


def setup_inputs(seed: int = 0) -> dict:
    key = jax.random.key(seed)
    ks = jax.random.split(key, 32)

    def nrm(k, shape, s):
        return jax.random.normal(k, shape, jnp.float32) * s

    return {
        'x_prompt': nrm(ks[0], (BATCH, SEQ, D_MODEL), 1.0),
        'x_sample': nrm(ks[1], (DEC_BATCH, DEC_SEQ, D_MODEL), 1.0),
        'cache_a_k': nrm(ks[2], (DEC_BATCH, DEPTH, PAST_LEN, A_HEADS, 2 * A_DQK), 1.0),
        'cache_a_v': nrm(ks[3], (DEC_BATCH, DEPTH, PAST_LEN, A_HEADS, A_DV), 1.0),
        'cache_b_k': nrm(ks[4], (DEC_BATCH, DEPTH, PAST_LEN, B_KV_HEADS, B_DH), 1.0),
        'cache_b_v': nrm(ks[5], (DEC_BATCH, DEPTH, PAST_LEN, B_KV_HEADS, B_DH), 1.0),
        'cache_c_kv': nrm(ks[6], (DEC_BATCH, DEPTH, PAST_LEN, C_KV_RANK), 1.0),
        'cache_c_kpe': nrm(ks[7], (DEC_BATCH, DEPTH, PAST_LEN, C_ROPE), 1.0),
        'c': nrm(ks[8], (DEC_BATCH, D_MODEL), 1.0),
        'c_ctx': nrm(ks[9], (D_MODEL,), 1.0),
        'w_mod': nrm(ks[10], (DEPTH, D_MODEL, N_MOD * D_MODEL), D_MODEL ** -0.5),
        'b_mod': nrm(ks[11], (DEPTH, N_MOD * D_MODEL), 0.02),
        'ln_g': 1.0 + nrm(ks[12], (DEPTH, 3, D_MODEL), 0.02),
        'ln_b': nrm(ks[13], (DEPTH, 3, D_MODEL), 0.02),
        'ffn_w1': nrm(ks[14], (DEPTH, 2, D_MODEL, D_FF), D_MODEL ** -0.5),
        'ffn_w3': nrm(ks[15], (DEPTH, 2, D_MODEL, D_FF), D_MODEL ** -0.5),
        'ffn_w2': nrm(ks[16], (DEPTH, 2, D_FF, D_MODEL), D_FF ** -0.5 * DN_BETA),
        'w_in': nrm(ks[17], (DEPTH, D_MODEL, IN_WIDTH), D_MODEL ** -0.5),
        'w_o': nrm(ks[18], (DEPTH, MIX_WIDTH, D_MODEL), MIX_WIDTH ** -0.5 * DN_BETA),
        'a_lambda': nrm(ks[19], (DEPTH, 4, A_DQK), 0.1),
        'a_subln_g': 1.0 + nrm(ks[20], (DEPTH, A_DV), 0.02),
        'b_sink': nrm(ks[21], (DEPTH, B_HEADS), 0.5),
        'c_q_norm_g': 1.0 + nrm(ks[22], (DEPTH, C_Q_RANK), 0.02),
        'c_w_q_up': nrm(ks[23], (DEPTH, C_Q_RANK, C_HEADS * (C_NOPE + C_ROPE)), C_Q_RANK ** -0.5),
        'c_kv_norm_g': 1.0 + nrm(ks[24], (DEPTH, C_KV_RANK), 0.02),
        'c_w_kv_up': nrm(ks[25], (DEPTH, C_KV_RANK, C_HEADS, C_NOPE + C_DV), C_KV_RANK ** -0.5),
    }


def reference(x_prompt, x_sample, cache_a_k, cache_a_v, cache_b_k, cache_b_v, cache_c_kv, cache_c_kpe,
              c, c_ctx, w_mod, b_mod, ln_g, ln_b, ffn_w1, ffn_w3, ffn_w2, w_in, w_o,
              a_lambda, a_subln_g, b_sink, c_q_norm_g, c_w_q_up, c_kv_norm_g, c_w_kv_up):
    n_lat = x_sample.shape[1]
    rows = n_lat // GRID_W
    ropes = (_axial_rope_tables(rows, A_DQK), _axial_rope_tables(rows, B_DH), _axial_rope_tables(rows, C_ROPE))
    xp = x_prompt
    xs = x_sample
    st_ak, st_av, st_bk, st_bv, st_ckv, st_ckpe = [], [], [], [], [], []
    for l in range(DEPTH):
        lam_init = 0.8 - 0.6 * math.exp(-0.3 * l)
        lam = _diff_lambda(a_lambda[l], lam_init)
        mod_p = _modulation(c_ctx[None, :], w_mod[l], b_mod[l])
        mod_s = _modulation(c, w_mod[l], b_mod[l])
        proj_w = (w_in[l], c_q_norm_g[l], c_w_q_up[l], c_kv_norm_g[l])
        mix_w = (lam, lam_init, a_subln_g[l], b_sink[l], c_w_kv_up[l], w_o[l])
        ffn1 = (ffn_w1[l, 0], ffn_w3[l, 0], ffn_w2[l, 0], ln_g[l, 0], ln_b[l, 0])
        ffn2 = (ffn_w1[l, 1], ffn_w3[l, 1], ffn_w2[l, 1], ln_g[l, 2], ln_b[l, 2])

        xp = _ffn_half(xp, mod_p, 0, *ffn1)
        tp = _project(_modulate(xp, mod_p[:, 3], mod_p[:, 4]), *proj_w, None)
        xp = _residual_post_norm(xp, _mix(tp, None, *mix_w), mod_p[:, 5], 1.0, ln_g[l, 1], ln_b[l, 1])
        s_ak, s_av, s_bk, s_bv, s_ckv, s_ckpe = _context_state(tp)
        st_ak.append(s_ak)
        st_av.append(s_av)
        st_bk.append(s_bk)
        st_bv.append(s_bv)
        st_ckv.append(s_ckv)
        st_ckpe.append(s_ckpe)
        xp = _ffn_half(xp, mod_p, 2, *ffn2)

        xs = _ffn_half(xs, mod_s, 0, *ffn1)
        ctx = (cache_a_k[:, l], cache_a_v[:, l], cache_b_k[:, l], cache_b_v[:, l],
               cache_c_kv[:, l], cache_c_kpe[:, l])
        ts = _project(_modulate(xs, mod_s[:, 3], mod_s[:, 4]), *proj_w, ropes)
        xs = _residual_post_norm(xs, _mix(ts, ctx, *mix_w), mod_s[:, 5], 1.0, ln_g[l, 1], ln_b[l, 1])
        xs = _ffn_half(xs, mod_s, 2, *ffn2)

    new_a_k = jnp.stack(st_ak, axis=1)
    new_a_v = jnp.stack(st_av, axis=1)
    new_b_k = jnp.stack(st_bk, axis=1)
    new_b_v = jnp.stack(st_bv, axis=1)
    new_c_kv = jnp.stack(st_ckv, axis=1)
    new_c_kpe = jnp.stack(st_ckpe, axis=1)
    return (xp, xs, new_a_k, new_a_v, new_b_k, new_b_v, new_c_kv, new_c_kpe)
```

```python
import functools
import math

import jax
import jax.numpy as jnp
import numpy as np
from jax import lax
from jax.experimental import pallas as pl
from jax.experimental.pallas import tpu as pltpu

D_MODEL = 1024
DEPTH = 2
GRID_W = 64
BLOCK = 128
A_HEADS = 4
A_DQK = 32
A_DV = 64
B_HEADS = 8
B_KV_HEADS = 2
B_GROUP = B_HEADS // B_KV_HEADS
B_DH = 64
WINDOW = 128
C_HEADS = 4
C_Q_RANK = 256
C_KV_RANK = 128
C_NOPE = 64
C_ROPE = 32
C_DV = 64
D_FF = 2816
N_MOD = 9
ROPE_BASE = 10000.0
LN_EPS = 1e-5
RMS_EPS = 1e-6
DN_ALPHA = (2 * DEPTH) ** 0.25
FFN_RES = 0.5
LOG2E = 1.4426950408889634

LANES = 128
MXU_DIM = 256
FF_CHUNK = MXU_DIM
N_FF_CHUNKS = D_FF // FF_CHUNK
TOKEN_TILE = 512
MOD_ROWS = 8
MOD_COL_TILE = 1024
KEY_CHUNK = 512
VMEM_LIMIT = 56 * 1024 * 1024
NEG_BIG = -1e30

OFF_AQ = 0
OFF_AK = 256
OFF_AV = 512
OFF_BQ = 768
OFF_BK4 = 1280
OFF_BV4 = 1792
OFF_CQD = 2304
OFF_CKVD = 2560
OFF_CKPE = 2688
W_EXT = 2816

BF16 = jnp.bfloat16
F32 = jnp.float32


def _dot(a, b):
    return jnp.dot(a, b, preferred_element_type=F32)


def _dot_nt(a, b):
    return lax.dot_general(a, b, (((1,), (1,)), ((), ())), preferred_element_type=F32)


def _cparams(n_grid):
    return pltpu.CompilerParams(dimension_semantics=("arbitrary",) * n_grid, vmem_limit_bytes=VMEM_LIMIT)


def _resident(shape):
    nd = len(shape)
    return pl.BlockSpec(shape, lambda *_: (0,) * nd, pipeline_mode=pl.Buffered(1))


def _layernorm(y, g, b):
    mu = jnp.mean(y, axis=-1, keepdims=True)
    d = y - mu
    var = jnp.mean(d * d, axis=-1, keepdims=True)
    return d * lax.rsqrt(var + LN_EPS) * g + b


def _mod_kernel(c_ref, w_ref, b_ref, o_ref):
    c = c_ref[...]
    s = (c / (1.0 + jnp.exp(-c))).astype(BF16)
    o_ref[...] = _dot(s, w_ref[...].astype(BF16)) + b_ref[...]


def _modulation(cond8, w_mod, b_mod):
    n_col = N_MOD * D_MODEL
    return pl.pallas_call(
        _mod_kernel,
        grid=(DEPTH, n_col // MOD_COL_TILE),
        in_specs=[
            pl.BlockSpec((MOD_ROWS, D_MODEL), lambda l, j: (0, 0)),
            pl.BlockSpec((None, D_MODEL, MOD_COL_TILE), lambda l, j: (l, 0, j)),
            pl.BlockSpec((None, 1, MOD_COL_TILE), lambda l, j: (l, 0, j)),
        ],
        out_specs=pl.BlockSpec((None, MOD_ROWS, MOD_COL_TILE), lambda l, j: (l, 0, j)),
        out_shape=jax.ShapeDtypeStruct((DEPTH, MOD_ROWS, n_col), F32),
        compiler_params=_cparams(2),
        name="modulation",
    )(cond8, w_mod, b_mod)


def _mod_spec(slot, row_fn):
    return pl.BlockSpec((None, None, 1, D_MODEL), lambda i, *_: (row_fn(i), slot, 0, 0))


def _ffn_kernel(x_ref, sh_ref, sc_ref, gt_ref, w1_ref, w3_ref, w2_ref, g_ref, b_ref, o_ref, h_scr, acc_scr):
    x = x_ref[...]
    h_scr[...] = (x * (1.0 + sc_ref[...]) + sh_ref[...]).astype(BF16)
    acc_scr[...] = jnp.zeros_like(acc_scr)

    def body(c, carry):
        h = h_scr[...]
        a = _dot(h, w1_ref[c])
        b = _dot(h, w3_ref[c])
        u = (a / (1.0 + jnp.exp(-a))) * b
        acc_scr[...] += _dot(u.astype(BF16), w2_ref[c])
        return carry

    lax.fori_loop(0, N_FF_CHUNKS, body, 0)
    y = DN_ALPHA * x + FFN_RES * gt_ref[...] * acc_scr[...]
    o_ref[...] = _layernorm(y, g_ref[...], b_ref[...])


def _ffn(x, mod_l, slot, w1c, w3c, w2c, g, b, row_fn):
    n = x.shape[0]
    tile = pl.BlockSpec((TOKEN_TILE, D_MODEL), lambda i: (i, 0))
    return pl.pallas_call(
        _ffn_kernel,
        grid=(n // TOKEN_TILE,),
        in_specs=[
            tile,
            _mod_spec(3 * slot, row_fn), _mod_spec(3 * slot + 1, row_fn), _mod_spec(3 * slot + 2, row_fn),
            _resident(w1c.shape), _resident(w3c.shape), _resident(w2c.shape),
            _resident(g.shape), _resident(b.shape),
        ],
        out_specs=tile,
        out_shape=jax.ShapeDtypeStruct((n, D_MODEL), F32),
        scratch_shapes=[pltpu.VMEM((TOKEN_TILE, D_MODEL), BF16), pltpu.VMEM((TOKEN_TILE, D_MODEL), F32)],
        compiler_params=_cparams(1),
        name="ffn_half",
    )(x, mod_l, mod_l, mod_l, w1c, w3c, w2c, g, b)


def _rope(x, cos, sin_lo, sin_hi, q):
    return x * cos + pltpu.roll(x, LANES - q, 1) * sin_lo + pltpu.roll(x, q, 1) * sin_hi


def _rope_cols(x, tabs, q):
    cos, sin_lo, sin_hi = tabs
    parts = [_rope(x[:, j * LANES:(j + 1) * LANES], cos, sin_lo, sin_hi, q) for j in range(x.shape[1] // LANES)]
    return parts[0] if len(parts) == 1 else jnp.concatenate(parts, axis=1)


def _rms(x, g):
    return x * lax.rsqrt(jnp.mean(x * x, axis=-1, keepdims=True) + RMS_EPS) * g


def _project_kernel(*refs, with_rope, with_state):
    (x_ref, sh_ref, sc_ref, win_ref, qg_ref, wqu_ref, wc2_ref, kvg_ref), rest = refs[:8], refs[8:]
    if with_rope:
        tab_refs, rest = rest[:6], rest[6:]
    (qa_ref, ka_ref, va_ref, qb_ref, kb_ref, vb_ref, qc_ref, kc_ref), rest = rest[:8], rest[8:]
    h_scr = rest[-1]
    rest = rest[:-1]

    h_scr[...] = (x_ref[...] * (1.0 + sc_ref[...]) + sh_ref[...]).astype(BF16)

    def piece(off, width):
        return _dot(h_scr[...], win_ref[:, off:off + width])

    a_q = piece(OFF_AQ, 256)
    a_k = piece(OFF_AK, 256)
    a_v = piece(OFF_AV, 256)
    b_q = piece(OFF_BQ, 512)
    b_k4 = piece(OFF_BK4, 512)
    b_v4 = piece(OFF_BV4, 512)
    c_qd = piece(OFF_CQD, 256)
    c_kvd = piece(OFF_CKVD, 128)
    c_kpe = piece(OFF_CKPE, 128)

    c_q = _dot(_rms(c_qd, qg_ref[...]).astype(BF16), wqu_ref[...])
    c_qn = c_q[:, :256]
    c_qp = c_q[:, 256:384]
    c_kv = _rms(c_kvd, kvg_ref[...])

    if with_state:
        sak_ref, sav_ref, sbk_ref, sbv_ref, sckv_ref, sckpe_ref = rest
        lane = lax.broadcasted_iota(jnp.int32, (1, LANES), 1)
        first_half = lane < B_DH
        sak_ref[...] = a_k
        sav_ref[...] = a_v
        sbk_ref[...] = jnp.where(first_half, b_k4[:, 0:128], b_k4[:, 256:384])
        sbv_ref[...] = jnp.where(first_half, b_v4[:, 0:128], b_v4[:, 256:384])
        sckv_ref[...] = c_kv
        sckpe_ref[...] = c_kpe[:, :C_ROPE]

    if with_rope:
        ta = tuple(r[...] for r in tab_refs[:3])
        tb = tuple(r[...] for r in tab_refs[3:])
        lane = lax.broadcasted_iota(jnp.int32, (1, LANES), 1)
        keep = (lane < C_ROPE).astype(F32)
        tck = tuple(v * keep for v in ta)
        a_q = _rope_cols(a_q, ta, A_DQK // 4)
        a_k = _rope_cols(a_k, ta, A_DQK // 4)
        b_q = _rope_cols(b_q, tb, B_DH // 4)
        b_k4 = _rope_cols(b_k4, tb, B_DH // 4)
        c_qp = _rope_cols(c_qp, ta, C_ROPE // 4)
        c_kpe = _rope_cols(c_kpe, tck, C_ROPE // 4)

    qa_ref[...] = (a_q * (A_DQK ** -0.5 * LOG2E)).astype(BF16)
    ka_ref[...] = a_k.astype(BF16)
    va_ref[...] = a_v.astype(BF16)
    qb_ref[...] = (b_q * (B_DH ** -0.5 * LOG2E)).astype(BF16)
    kb_ref[...] = b_k4.astype(BF16)
    vb_ref[...] = b_v4.astype(BF16)
    c_scale = (C_NOPE + C_ROPE) ** -0.5 * LOG2E
    c_qcat = jnp.concatenate([c_qn * c_scale, c_qp * c_scale], axis=1).astype(BF16)
    qc_ref[...] = _dot(c_qcat, wc2_ref[...]).astype(BF16)
    kc_ref[...] = jnp.concatenate([c_kv, c_kpe], axis=1).astype(BF16)


def _project(x, mod_l, w_in_ext, qg, wqu, wc2, kvg, row_fn, tabs=None, with_state=False):
    n = x.shape[0]
    with_rope = tabs is not None

    def rows(width):
        return pl.BlockSpec((TOKEN_TILE, width), lambda i: (i, 0))

    in_specs = [rows(D_MODEL), _mod_spec(3, row_fn), _mod_spec(4, row_fn),
                _resident(w_in_ext.shape), _resident(qg.shape), _resident(wqu.shape),
                _resident(wc2.shape), _resident(kvg.shape)]
    args = [x, mod_l, mod_l, w_in_ext, qg, wqu, wc2, kvg]
    if with_rope:
        tiles_per_seq = tabs[0].shape[0] // TOKEN_TILE
        in_specs += [pl.BlockSpec((TOKEN_TILE, LANES), lambda i: (i % tiles_per_seq, 0))] * 6
        args += list(tabs)
    widths = [256, 256, 256, 512, 512, 512, 1024, 256]
    out_specs = [rows(w) for w in widths]
    out_shape = [jax.ShapeDtypeStruct((n, w), BF16) for w in widths]
    if with_state:
        swidths = [256, 256, 128, 128, 128, C_ROPE]
        out_specs += [rows(w) for w in swidths]
        out_shape += [jax.ShapeDtypeStruct((n, w), F32) for w in swidths]
    return pl.pallas_call(
        functools.partial(_project_kernel, with_rope=with_rope, with_state=with_state),
        grid=(n // TOKEN_TILE,),
        in_specs=in_specs,
        out_specs=out_specs,
        out_shape=out_shape,
        scratch_shapes=[pltpu.VMEM((TOKEN_TILE, D_MODEL), BF16)],
        compiler_params=_cparams(1),
        name="project_latent" if with_rope else "project_context",
    )(*args)


def _lane_mask(width, lo, size, dtype):
    lane = lax.broadcasted_iota(jnp.int32, (1, width), 1)
    return ((lane >= lo) & (lane < lo + size)).astype(dtype)


def _fold_lanes(x, op):
    r = x[:, :LANES]
    for j in range(1, x.shape[1] // LANES):
        r = op(r, x[:, j * LANES:(j + 1) * LANES])
    return r


def _diff_lambda(lp_ref, lam_init):
    lp = lp_ref[...]
    s1 = jnp.sum(lp[0:1, :] * lp[1:2, :], axis=-1, keepdims=True)
    s2 = jnp.sum(lp[2:3, :] * lp[3:4, :], axis=-1, keepdims=True)
    return jnp.exp(s1) - jnp.exp(s2) + lam_init


def _subln_accumulate(o_a, o_full, head, tq):
    mask = _lane_mask(256, head * A_DV, A_DV, F32)
    om = o_full * mask
    ms = jnp.sum(om * om, axis=-1, keepdims=True) * (1.0 / A_DV)
    return o_a + om * lax.rsqrt(ms + RMS_EPS)


def _finish(o_parts, x_ref, gt_ref, wo_ref, g_ref, b_ref, o_ref):
    o = jnp.concatenate(o_parts, axis=1).astype(BF16)
    f = _dot(o, wo_ref[...])
    y = DN_ALPHA * x_ref[...] + gt_ref[...] * f
    o_ref[...] = _layernorm(y, g_ref[...], b_ref[...])


def _attn_ctx_kernel(qa_ref, ka_ref, va_ref, qb_ref, kb_ref, vb_ref, qc_ref, kc_ref,
                     x_ref, gt_ref, wo_ref, g_ref, b_ref, lp_ref, sg_ref, sink_ref, wuv_ref,
                     o_ref, *, lam_init):
    tq = qa_ref.shape[0]
    lam = _diff_lambda(lp_ref, lam_init)

    def softmax_parts(s):
        m = jnp.max(s, axis=-1, keepdims=True)
        e = jnp.exp2(s - m)
        return e, jnp.sum(e, axis=-1, keepdims=True)

    qa = qa_ref[...]
    ka = ka_ref[...]
    va = va_ref[...]
    o_a = jnp.zeros((tq, 256), F32)
    for h in range(A_HEADS):
        q2 = jnp.concatenate([qa * _lane_mask(256, h * 64 + j * A_DQK, A_DQK, BF16) for j in range(2)], axis=0)
        e, l = softmax_parts(_dot_nt(q2, ka))
        acc = _dot(e.astype(BF16), va) / l
        o_a = _subln_accumulate(o_a, acc[:tq] - lam * acc[tq:], h, tq)
    o_a = o_a * sg_ref[...]

    qb = qb_ref[...]
    kb = kb_ref[...]
    vb = vb_ref[...]
    o_b = []
    for c in range(B_KV_HEADS):
        cs = slice(256 * c, 256 * c + 256)
        masks = [_lane_mask(256, g * B_DH, B_DH, BF16) for g in range(B_GROUP)]
        q4 = jnp.concatenate([qb[:, cs] * mk for mk in masks], axis=0)
        sink = jnp.concatenate([jnp.full((tq, 1), sink_ref[c * B_GROUP + g] * LOG2E, F32)
                                for g in range(B_GROUP)], axis=0)
        s = _dot_nt(q4, kb[:, cs])
        m = jnp.maximum(jnp.max(s, axis=-1, keepdims=True), sink)
        e = jnp.exp2(s - m)
        den = jnp.sum(e, axis=-1, keepdims=True) + jnp.exp2(sink - m)
        acc = _dot(e.astype(BF16), vb[:, cs]) / den
        oc = acc[:tq] * masks[0].astype(F32)
        for g in range(1, B_GROUP):
            oc = oc + acc[g * tq:(g + 1) * tq] * masks[g].astype(F32)
        o_b.append(oc)

    qc = qc_ref[...]
    kc = kc_ref[...]
    q4 = jnp.concatenate([qc[:, 256 * h:256 * h + 256] for h in range(C_HEADS)], axis=0)
    e, l = softmax_parts(_dot_nt(q4, kc))
    acc = _dot(e.astype(BF16), kc) / l
    o_lat = jnp.concatenate([acc[h * tq:(h + 1) * tq, :C_KV_RANK] for h in range(C_HEADS)], axis=1)
    o_c = _dot(o_lat.astype(BF16), wuv_ref[...])

    _finish([o_a] + o_b + [o_c], x_ref, gt_ref, wo_ref, g_ref, b_ref, o_ref)


def _attn_ctx(proj, x, mod_l, wo, g, b, lp, sg, sink, wuv, lam_init, seq):
    qa, ka, va, qb, kb, vb, qc, kc = proj
    n = x.shape[0]

    def rows(width):
        return pl.BlockSpec((seq, width), lambda i: (i, 0))

    return pl.pallas_call(
        functools.partial(_attn_ctx_kernel, lam_init=lam_init),
        grid=(n // seq,),
        in_specs=[rows(256), rows(256), rows(256), rows(512), rows(512), rows(512), rows(1024), rows(256),
                  rows(D_MODEL), _mod_spec(5, lambda i: 0),
                  _resident(wo.shape), _resident(g.shape), _resident(b.shape), _resident(lp.shape),
                  _resident(sg.shape), pl.BlockSpec(memory_space=pltpu.SMEM), _resident(wuv.shape)],
        out_specs=rows(D_MODEL),
        out_shape=jax.ShapeDtypeStruct((n, D_MODEL), F32),
        compiler_params=_cparams(1),
        name="attention_context",
    )(qa, ka, va, qb, kb, vb, qc, kc, x, mod_l, wo, g, b, lp, sg, sink, wuv)


def _attn_lat_kernel(qa_ref, qb_ref, qc_ref,
                     ka_ref, va_ref, kb_ref, vb_ref, kc_ref,
                     xka_ref, xva_ref, xkb_ref, xvb_ref, xkc_ref,
                     x_ref, gt_ref, wo_ref, g_ref, b_ref, lp_ref, sg_ref, sink_ref, wuv_ref,
                     o_ref, s_scr, *, lam_init, n_lat):
    tq = qa_ref.shape[0]
    n_chunks = n_lat // KEY_CHUNK
    lam = _diff_lambda(lp_ref, lam_init)

    def full_softmax_pv(q, k_ref, xk_ref, v_ref, xv_ref):
        s_ctx = _dot_nt(q, xk_ref[...])
        m_run = _fold_lanes(s_ctx, jnp.maximum)

        def pass1(c, m_run):
            start = pl.multiple_of(c * KEY_CHUNK, KEY_CHUNK)
            s = _dot_nt(q, k_ref[pl.ds(start, KEY_CHUNK), :])
            s_scr[c] = s
            return jnp.maximum(m_run, _fold_lanes(s, jnp.maximum))

        m_run = lax.fori_loop(0, n_chunks, pass1, m_run)
        m = jnp.max(m_run, axis=-1, keepdims=True)

        e = jnp.exp2(s_ctx - m)
        l_run = _fold_lanes(e, jnp.add)
        acc = _dot(e.astype(BF16), xv_ref[...])

        def pass2(c, carry):
            l_run, acc = carry
            start = pl.multiple_of(c * KEY_CHUNK, KEY_CHUNK)
            e = jnp.exp2(s_scr[c] - m)
            acc = acc + _dot(e.astype(BF16), v_ref[pl.ds(start, KEY_CHUNK), :])
            return l_run + _fold_lanes(e, jnp.add), acc

        l_run, acc = lax.fori_loop(0, n_chunks, pass2, (l_run, acc))
        return acc, jnp.sum(l_run, axis=-1, keepdims=True)

    qa = qa_ref[...]
    o_a = jnp.zeros((tq, 256), F32)
    for h in range(A_HEADS):
        q2 = jnp.concatenate([qa * _lane_mask(256, h * 64 + j * A_DQK, A_DQK, BF16) for j in range(2)], axis=0)
        acc, l = full_softmax_pv(q2, ka_ref, xka_ref, va_ref, xva_ref)
        acc = acc / l
        o_a = _subln_accumulate(o_a, acc[:tq] - lam * acc[tq:], h, tq)
    o_a = o_a * sg_ref[...]

    i = pl.program_id(1)
    q0 = i * tq
    win = tq + 2 * WINDOW
    start = pl.multiple_of(jnp.clip(q0 - WINDOW, 0, n_lat - win), BLOCK)
    kpos = start + lax.broadcasted_iota(jnp.int32, (1, win), 1)
    qpos = q0 + (lax.broadcasted_iota(jnp.int32, (B_GROUP * tq, 1), 0) & (tq - 1))
    valid = jnp.abs(kpos - qpos) <= WINDOW
    qb = qb_ref[...]
    o_b = []
    for c in range(B_KV_HEADS):
        cs = slice(256 * c, 256 * c + 256)
        masks = [_lane_mask(256, g * B_DH, B_DH, BF16) for g in range(B_GROUP)]
        q4 = jnp.concatenate([qb[:, cs] * mk for mk in masks], axis=0)
        sink = jnp.concatenate([jnp.full((tq, 1), sink_ref[c * B_GROUP + g] * LOG2E, F32)
                                for g in range(B_GROUP)], axis=0)
        s_ctx = _dot_nt(q4, xkb_ref[:, cs])
        s_loc = jnp.where(valid, _dot_nt(q4, kb_ref[pl.ds(start, win), cs]), NEG_BIG)
        m = jnp.maximum(jnp.maximum(jnp.max(s_ctx, axis=-1, keepdims=True),
                                    jnp.max(s_loc, axis=-1, keepdims=True)), sink)
        e_ctx = jnp.exp2(s_ctx - m)
        e_loc = jnp.exp2(s_loc - m)
        den = (jnp.sum(e_ctx, axis=-1, keepdims=True) + jnp.sum(e_loc, axis=-1, keepdims=True)
               + jnp.exp2(sink - m))
        acc = (_dot(e_ctx.astype(BF16), xvb_ref[:, cs])
               + _dot(e_loc.astype(BF16), vb_ref[pl.ds(start, win), cs])) / den
        oc = acc[:tq] * masks[0].astype(F32)
        for g in range(1, B_GROUP):
            oc = oc + acc[g * tq:(g + 1) * tq] * masks[g].astype(F32)
        o_b.append(oc)

    qc = qc_ref[...]
    lat = []
    for hp in range(C_HEADS // 2):
        q2 = jnp.concatenate([qc[:, 256 * (2 * hp + j):256 * (2 * hp + j) + 256] for j in range(2)], axis=0)
        acc, l = full_softmax_pv(q2, kc_ref, xkc_ref, kc_ref, xkc_ref)
        acc = acc / l
        lat += [acc[:tq, :C_KV_RANK], acc[tq:, :C_KV_RANK]]
    o_c = _dot(jnp.concatenate(lat, axis=1).astype(BF16), wuv_ref[...])

    _finish([o_a] + o_b + [o_c], x_ref, gt_ref, wo_ref, g_ref, b_ref, o_ref)


def _attn_lat(proj, ctx, x, mod_l, wo, g, b, lp, sg, sink, wuv, lam_init, n_batch, n_lat):
    qa, ka, va, qb, kb, vb, qc, kc = proj
    xka, xva, xkb, xvb, xkc = ctx
    tq = BLOCK
    tiles = n_lat // tq
    n_ctx = xka.shape[1]

    def qrows(width):
        return pl.BlockSpec((tq, width), lambda bi, i: (bi * tiles + i, 0))

    def seq(width):
        return pl.BlockSpec((n_lat, width), lambda bi, i: (bi, 0), pipeline_mode=pl.Buffered(1))

    def cseq(width):
        return pl.BlockSpec((None, n_ctx, width), lambda bi, i: (bi, 0, 0))

    def res(a):
        nd = a.ndim
        return pl.BlockSpec(a.shape, lambda bi, i: (0,) * nd, pipeline_mode=pl.Buffered(1))

    return pl.pallas_call(
        functools.partial(_attn_lat_kernel, lam_init=lam_init, n_lat=n_lat),
        grid=(n_batch, tiles),
        in_specs=[qrows(256), qrows(512), qrows(1024),
                  seq(256), seq(256), seq(512), seq(512), seq(256),
                  cseq(256), cseq(256), cseq(512), cseq(512), cseq(256),
                  qrows(D_MODEL),
                  pl.BlockSpec((None, None, 1, D_MODEL), lambda bi, i: (1 + bi, 5, 0, 0)),
                  res(wo), res(g), res(b), res(lp), res(sg),
                  pl.BlockSpec(memory_space=pltpu.SMEM), res(wuv)],
        out_specs=qrows(D_MODEL),
        out_shape=jax.ShapeDtypeStruct((n_batch * n_lat, D_MODEL), F32),
        scratch_shapes=[pltpu.VMEM((n_lat // KEY_CHUNK, 2 * tq, KEY_CHUNK), F32)],
        compiler_params=_cparams(2),
        name="attention_latent",
    )(qa, qb, qc, ka, va, kb, vb, kc, xka, xva, xkb, xvb, xkc, x, mod_l, wo, g, b, lp, sg, sink, wuv)


def _rope_tables(rows, dim):
    row = jnp.repeat(jnp.arange(rows, dtype=F32), GRID_W)
    col = jnp.tile(jnp.arange(GRID_W, dtype=F32), rows)
    a = dim // 2
    inv = jnp.power(ROPE_BASE, -jnp.arange(0, a, 2, dtype=F32) / a)
    ar = row[:, None] * inv[None, :]
    ac = col[:, None] * inv[None, :]
    ang = jnp.concatenate([ar, ar, ac, ac], axis=-1)
    reps = LANES // dim
    cos = jnp.tile(jnp.cos(ang), (1, reps))
    sin = jnp.tile(jnp.sin(ang), (1, reps))
    lower = (np.arange(LANES) % (dim // 2)) < (dim // 4)
    sin_lo = jnp.where(lower[None, :], -sin, 0.0)
    sin_hi = jnp.where(lower[None, :], 0.0, sin)
    return cos, sin_lo, sin_hi


def _extend_w_in(w_in_l):
    o = np.cumsum([0, 256, 256, 256, 512, 128, 128, 256, 128, 32])
    a_q, a_k, a_v, b_q, b_k, b_v, c_qd, c_kvd, c_kpe = [w_in_l[:, o[i]:o[i + 1]] for i in range(9)]

    def rep(w):
        return jnp.concatenate([w[:, B_DH * c:B_DH * (c + 1)] for c in range(B_KV_HEADS) for _ in range(B_GROUP)], axis=1)

    pad = jnp.zeros((D_MODEL, W_EXT - OFF_CKPE - C_ROPE), w_in_l.dtype)
    return jnp.concatenate([a_q, a_k, a_v, b_q, rep(b_k), rep(b_v), c_qd, c_kvd, c_kpe, pad], axis=1).astype(BF16)


def _c_weights(w_q_up_l, w_kv_up_l):
    wq = w_q_up_l.reshape(C_Q_RANK, C_HEADS, C_NOPE + C_ROPE)
    wqu = jnp.concatenate([wq[:, :, :C_NOPE].reshape(C_Q_RANK, -1), wq[:, :, C_NOPE:].reshape(C_Q_RANK, -1)], axis=1)
    w_uk = w_kv_up_l[:, :, :C_NOPE]
    w_uv = w_kv_up_l[:, :, C_NOPE:]
    wc2 = jnp.zeros((C_HEADS * C_NOPE + C_HEADS * C_ROPE, C_HEADS * 256), F32)
    wuv = jnp.zeros((C_HEADS * C_KV_RANK, C_HEADS * C_DV), F32)
    eye = jnp.eye(C_ROPE, dtype=F32)
    for h in range(C_HEADS):
        wc2 = wc2.at[C_NOPE * h:C_NOPE * (h + 1), 256 * h:256 * h + C_KV_RANK].set(w_uk[:, h, :].T)
        r0 = C_HEADS * C_NOPE + C_ROPE * h
        wc2 = wc2.at[r0:r0 + C_ROPE, 256 * h + C_KV_RANK:256 * h + C_KV_RANK + C_ROPE].set(eye)
        wuv = wuv.at[C_KV_RANK * h:C_KV_RANK * (h + 1), C_DV * h:C_DV * (h + 1)].set(w_uv[:, h, :])
    return wqu.astype(BF16), wc2.astype(BF16), wuv.astype(BF16)


def _ffn_weights(w1, w3, w2):
    w1c = w1.reshape(D_MODEL, N_FF_CHUNKS, FF_CHUNK).transpose(1, 0, 2).astype(BF16)
    w3c = w3.reshape(D_MODEL, N_FF_CHUNKS, FF_CHUNK).transpose(1, 0, 2).astype(BF16)
    w2c = w2.reshape(N_FF_CHUNKS, FF_CHUNK, D_MODEL).astype(BF16)
    return w1c, w3c, w2c


def _rep_kv(t):
    b, n = t.shape[:2]
    return jnp.broadcast_to(t[:, :, :, None, :], (b, n, B_KV_HEADS, B_GROUP, B_DH)).reshape(b, n, -1)


def kernel(x_prompt, x_sample, cache_a_k, cache_a_v, cache_b_k, cache_b_v, cache_c_kv, cache_c_kpe, c, c_ctx,
           w_mod, b_mod, ln_g, ln_b, ffn_w1, ffn_w3, ffn_w2, w_in, w_o, a_lambda, a_subln_g, b_sink,
           c_q_norm_g, c_w_q_up, c_kv_norm_g, c_w_kv_up):
    n_req, seq, _ = x_prompt.shape
    n_dec, n_lat, _ = x_sample.shape
    n_ctx = cache_a_k.shape[2]
    assert n_lat % KEY_CHUNK == 0 and n_lat % TOKEN_TILE == 0 and (n_req * seq) % TOKEN_TILE == 0
    assert 1 + n_dec <= MOD_ROWS

    cond8 = jnp.concatenate([c_ctx[None, :], c, jnp.zeros((MOD_ROWS - 1 - n_dec, D_MODEL), F32)], axis=0)
    mod = _modulation(cond8, w_mod, b_mod[:, None, :]).reshape(DEPTH, MOD_ROWS, N_MOD, 1, D_MODEL)

    tabs = _rope_tables(n_lat // GRID_W, A_DQK) + _rope_tables(n_lat // GRID_W, B_DH)
    tiles_per_lat = n_lat // TOKEN_TILE

    def prompt_row(i):
        return 0

    def sample_row(i):
        return 1 + i // tiles_per_lat

    xp = x_prompt.reshape(n_req * seq, D_MODEL)
    xs = x_sample.reshape(n_dec * n_lat, D_MODEL)
    states = []
    for l in range(DEPTH):
        lam_init = 0.8 - 0.6 * math.exp(-0.3 * l)
        mod_l = mod[l]
        ffn1 = _ffn_weights(ffn_w1[l, 0], ffn_w3[l, 0], ffn_w2[l, 0])
        ffn2 = _ffn_weights(ffn_w1[l, 1], ffn_w3[l, 1], ffn_w2[l, 1])
        g0, g1, g2 = (ln_g[l, k][None, :] for k in range(3))
        b0, b1, b2 = (ln_b[l, k][None, :] for k in range(3))
        w_in_ext = _extend_w_in(w_in[l])
        wqu, wc2, wuv = _c_weights(c_w_q_up[l], c_w_kv_up[l])
        qg = c_q_norm_g[l][None, :]
        kvg = c_kv_norm_g[l][None, :]
        wo = w_o[l].astype(BF16)
        sg = jnp.tile(a_subln_g[l], A_HEADS)[None, :] * (1.0 - lam_init)
        attn_w = (wo, g1, b1, a_lambda[l], sg, b_sink[l], wuv)

        xp = _ffn(xp, mod_l, 0, *ffn1, g0, b0, prompt_row)
        outs = _project(xp, mod_l, w_in_ext, qg, wqu, wc2, kvg, prompt_row, with_state=True)
        states.append(outs[8:])
        xp = _attn_ctx(outs[:8], xp, mod_l, *attn_w, lam_init, seq)
        xp = _ffn(xp, mod_l, 2, *ffn2, g2, b2, prompt_row)

        xs = _ffn(xs, mod_l, 0, *ffn1, g0, b0, sample_row)
        outs = _project(xs, mod_l, w_in_ext, qg, wqu, wc2, kvg, sample_row, tabs=tabs)
        zeros = jnp.zeros((n_dec, n_ctx, 256 - C_KV_RANK - C_ROPE), F32)
        ctx = (cache_a_k[:, l].reshape(n_dec, n_ctx, -1), cache_a_v[:, l].reshape(n_dec, n_ctx, -1),
               _rep_kv(cache_b_k[:, l]), _rep_kv(cache_b_v[:, l]),
               jnp.concatenate([cache_c_kv[:, l], cache_c_kpe[:, l], zeros], axis=-1))
        ctx = tuple(t.astype(BF16) for t in ctx)
        xs = _attn_lat(outs, ctx, xs, mod_l, *attn_w, lam_init, n_dec, n_lat)
        xs = _ffn(xs, mod_l, 2, *ffn2, g2, b2, sample_row)

    def stacked(k, tail):
        return jnp.stack([states[l][k].reshape((n_req, seq) + tail) for l in range(DEPTH)], axis=1)

    return (xp.reshape(n_req, seq, D_MODEL), xs.reshape(n_dec, n_lat, D_MODEL),
            stacked(0, (A_HEADS, 2 * A_DQK)), stacked(1, (A_HEADS, A_DV)),
            stacked(2, (B_KV_HEADS, B_DH)), stacked(3, (B_KV_HEADS, B_DH)),
            stacked(4, (C_KV_RANK,)), stacked(5, (C_ROPE,)))
```

```python
import functools
import math

import jax
import jax.numpy as jnp
import numpy as np
from jax import lax
from jax.experimental import pallas as pl
from jax.experimental.pallas import tpu as pltpu

D_MODEL = 1024
DEPTH = 2
GRID_W = 64
BLOCK = 128
A_HEADS = 4
A_DQK = 32
A_DV = 64
B_HEADS = 8
B_KV_HEADS = 2
B_GROUP = B_HEADS // B_KV_HEADS
B_DH = 64
WINDOW = 128
C_HEADS = 4
C_Q_RANK = 256
C_KV_RANK = 128
C_NOPE = 64
C_ROPE = 32
C_DV = 64
D_FF = 2816
N_MOD = 9
ROPE_BASE = 10000.0
LN_EPS = 1e-5
RMS_EPS = 1e-6
DN_ALPHA = (2 * DEPTH) ** 0.25
FFN_RES = 0.5
LOG2E = 1.4426950408889634

LANES = 128
MXU_DIM = 256
FF_CHUNK = MXU_DIM
N_FF_CHUNKS = D_FF // FF_CHUNK
TOKEN_TILE = 512
MOD_ROWS = 8
MOD_COL_TILE = 1024
KEY_CHUNK = 512
VMEM_LIMIT = 56 * 1024 * 1024
NEG_BIG = -1e30

OFF_AQ = 0
OFF_AK = 256
OFF_AV = 512
OFF_BQ = 768
OFF_BK4 = 1280
OFF_BV4 = 1792
OFF_CQD = 2304
OFF_CKVD = 2560
OFF_CKPE = 2688
W_EXT = 2816

BF16 = jnp.bfloat16
F32 = jnp.float32


def _dot(a, b):
    return jnp.dot(a, b, preferred_element_type=F32)


def _dot_nt(a, b):
    return lax.dot_general(a, b, (((1,), (1,)), ((), ())), preferred_element_type=F32)


def _cparams(n_grid):
    return pltpu.CompilerParams(dimension_semantics=("arbitrary",) * n_grid, vmem_limit_bytes=VMEM_LIMIT)


def _resident(shape):
    nd = len(shape)
    return pl.BlockSpec(shape, lambda *_: (0,) * nd, pipeline_mode=pl.Buffered(1))


def _layernorm(y, g, b):
    mu = jnp.mean(y, axis=-1, keepdims=True)
    d = y - mu
    var = jnp.mean(d * d, axis=-1, keepdims=True)
    return d * lax.rsqrt(var + LN_EPS) * g + b


def _mod_kernel(c_ref, w_ref, b_ref, o_ref):
    c = c_ref[...]
    s = (c / (1.0 + jnp.exp(-c))).astype(BF16)
    o_ref[...] = _dot(s, w_ref[...].astype(BF16)) + b_ref[...]


def _modulation(cond8, w_mod, b_mod):
    n_col = N_MOD * D_MODEL
    return pl.pallas_call(
        _mod_kernel,
        grid=(DEPTH, n_col // MOD_COL_TILE),
        in_specs=[
            pl.BlockSpec((MOD_ROWS, D_MODEL), lambda l, j: (0, 0)),
            pl.BlockSpec((None, D_MODEL, MOD_COL_TILE), lambda l, j: (l, 0, j)),
            pl.BlockSpec((None, 1, MOD_COL_TILE), lambda l, j: (l, 0, j)),
        ],
        out_specs=pl.BlockSpec((None, MOD_ROWS, MOD_COL_TILE), lambda l, j: (l, 0, j)),
        out_shape=jax.ShapeDtypeStruct((DEPTH, MOD_ROWS, n_col), F32),
        compiler_params=_cparams(2),
        name="modulation",
    )(cond8, w_mod, b_mod)


def _mod_spec(slot, row_fn):
    return pl.BlockSpec((None, None, 1, D_MODEL), lambda i, *_: (row_fn(i), slot, 0, 0))


def _ffn_kernel(x_ref, sh_ref, sc_ref, gt_ref, w1_ref, w3_ref, w2_ref, g_ref, b_ref, o_ref, h_scr, acc_scr):
    x = x_ref[...]
    h_scr[...] = (x * (1.0 + sc_ref[...]) + sh_ref[...]).astype(BF16)
    acc_scr[...] = jnp.zeros_like(acc_scr)

    def body(c, carry):
        h = h_scr[...]
        a = _dot(h, w1_ref[c])
        b = _dot(h, w3_ref[c])
        u = (a / (1.0 + jnp.exp(-a))) * b
        acc_scr[...] += _dot(u.astype(BF16), w2_ref[c])
        return carry

    lax.fori_loop(0, N_FF_CHUNKS, body, 0)
    y = DN_ALPHA * x + FFN_RES * gt_ref[...] * acc_scr[...]
    o_ref[...] = _layernorm(y, g_ref[...], b_ref[...])


def _ffn(x, mod_l, slot, w1c, w3c, w2c, g, b, row_fn):
    n = x.shape[0]
    tile = pl.BlockSpec((TOKEN_TILE, D_MODEL), lambda i: (i, 0))
    return pl.pallas_call(
        _ffn_kernel,
        grid=(n // TOKEN_TILE,),
        in_specs=[
            tile,
            _mod_spec(3 * slot, row_fn), _mod_spec(3 * slot + 1, row_fn), _mod_spec(3 * slot + 2, row_fn),
            _resident(w1c.shape), _resident(w3c.shape), _resident(w2c.shape),
            _resident(g.shape), _resident(b.shape),
        ],
        out_specs=tile,
        out_shape=jax.ShapeDtypeStruct((n, D_MODEL), F32),
        scratch_shapes=[pltpu.VMEM((TOKEN_TILE, D_MODEL), BF16), pltpu.VMEM((TOKEN_TILE, D_MODEL), F32)],
        compiler_params=_cparams(1),
        name="ffn_half",
    )(x, mod_l, mod_l, mod_l, w1c, w3c, w2c, g, b)


def _rope(x, cos, sin_lo, sin_hi, q):
    return x * cos + pltpu.roll(x, LANES - q, 1) * sin_lo + pltpu.roll(x, q, 1) * sin_hi


def _rope_cols(x, tabs, q):
    cos, sin_lo, sin_hi = tabs
    parts = [_rope(x[:, j * LANES:(j + 1) * LANES], cos, sin_lo, sin_hi, q) for j in range(x.shape[1] // LANES)]
    return parts[0] if len(parts) == 1 else jnp.concatenate(parts, axis=1)


def _rms(x, g):
    return x * lax.rsqrt(jnp.mean(x * x, axis=-1, keepdims=True) + RMS_EPS) * g


def _project_kernel(*refs, with_rope, with_state):
    (x_ref, sh_ref, sc_ref, win_ref, qg_ref, wqu_ref, wc2_ref, kvg_ref), rest = refs[:8], refs[8:]
    if with_rope:
        tab_refs, rest = rest[:6], rest[6:]
    (qa_ref, ka_ref, va_ref, qb_ref, kb_ref, vb_ref, qc_ref, kc_ref), rest = rest[:8], rest[8:]
    h_scr = rest[-1]
    rest = rest[:-1]

    h_scr[...] = (x_ref[...] * (1.0 + sc_ref[...]) + sh_ref[...]).astype(BF16)

    def piece(off, width):
        return _dot(h_scr[...], win_ref[:, off:off + width])

    a_q = piece(OFF_AQ, 256)
    a_k = piece(OFF_AK, 256)
    a_v = piece(OFF_AV, 256)
    b_q = piece(OFF_BQ, 512)
    b_k4 = piece(OFF_BK4, 512)
    b_v4 = piece(OFF_BV4, 512)
    c_qd = piece(OFF_CQD, 256)
    c_kvd = piece(OFF_CKVD, 128)
    c_kpe = piece(OFF_CKPE, 128)

    c_q = _dot(_rms(c_qd, qg_ref[...]).astype(BF16), wqu_ref[...])
    c_qn = c_q[:, :256]
    c_qp = c_q[:, 256:384]
    c_kv = _rms(c_kvd, kvg_ref[...])

    if with_state:
        sak_ref, sav_ref, sbk_ref, sbv_ref, sckv_ref, sckpe_ref = rest
        lane = lax.broadcasted_iota(jnp.int32, (1, LANES), 1)
        first_half = lane < B_DH
        sak_ref[...] = a_k
        sav_ref[...] = a_v
        sbk_ref[...] = jnp.where(first_half, b_k4[:, 0:128], b_k4[:, 256:384])
        sbv_ref[...] = jnp.where(first_half, b_v4[:, 0:128], b_v4[:, 256:384])
        sckv_ref[...] = c_kv
        sckpe_ref[...] = c_kpe[:, :C_ROPE]

    if with_rope:
        ta = tuple(r[...] for r in tab_refs[:3])
        tb = tuple(r[...] for r in tab_refs[3:])
        lane = lax.broadcasted_iota(jnp.int32, (1, LANES), 1)
        keep = (lane < C_ROPE).astype(F32)
        tck = tuple(v * keep for v in ta)
        a_q = _rope_cols(a_q, ta, A_DQK // 4)
        a_k = _rope_cols(a_k, ta, A_DQK // 4)
        b_q = _rope_cols(b_q, tb, B_DH // 4)
        b_k4 = _rope_cols(b_k4, tb, B_DH // 4)
        c_qp = _rope_cols(c_qp, ta, C_ROPE // 4)
        c_kpe = _rope_cols(c_kpe, tck, C_ROPE // 4)

    qa_ref[...] = (a_q * (A_DQK ** -0.5 * LOG2E)).astype(BF16)
    ka_ref[...] = a_k.astype(BF16)
    va_ref[...] = a_v.astype(BF16)
    qb_ref[...] = (b_q * (B_DH ** -0.5 * LOG2E)).astype(BF16)
    kb_ref[...] = b_k4.astype(BF16)
    vb_ref[...] = b_v4.astype(BF16)
    c_scale = (C_NOPE + C_ROPE) ** -0.5 * LOG2E
    c_qcat = jnp.concatenate([c_qn * c_scale, c_qp * c_scale], axis=1).astype(BF16)
    qc_ref[...] = _dot(c_qcat, wc2_ref[...]).astype(BF16)
    kc_ref[...] = jnp.concatenate([c_kv, c_kpe], axis=1).astype(BF16)


def _project(x, mod_l, w_in_ext, qg, wqu, wc2, kvg, row_fn, tabs=None, with_state=False):
    n = x.shape[0]
    with_rope = tabs is not None

    def rows(width):
        return pl.BlockSpec((TOKEN_TILE, width), lambda i: (i, 0))

    in_specs = [rows(D_MODEL), _mod_spec(3, row_fn), _mod_spec(4, row_fn),
                _resident(w_in_ext.shape), _resident(qg.shape), _resident(wqu.shape),
                _resident(wc2.shape), _resident(kvg.shape)]
    args = [x, mod_l, mod_l, w_in_ext, qg, wqu, wc2, kvg]
    if with_rope:
        tiles_per_seq = tabs[0].shape[0] // TOKEN_TILE
        in_specs += [pl.BlockSpec((TOKEN_TILE, LANES), lambda i: (i % tiles_per_seq, 0))] * 6
        args += list(tabs)
    widths = [256, 256, 256, 512, 512, 512, 1024, 256]
    out_specs = [rows(w) for w in widths]
    out_shape = [jax.ShapeDtypeStruct((n, w), BF16) for w in widths]
    if with_state:
        swidths = [256, 256, 128, 128, 128, C_ROPE]
        out_specs += [rows(w) for w in swidths]
        out_shape += [jax.ShapeDtypeStruct((n, w), F32) for w in swidths]
    return pl.pallas_call(
        functools.partial(_project_kernel, with_rope=with_rope, with_state=with_state),
        grid=(n // TOKEN_TILE,),
        in_specs=in_specs,
        out_specs=out_specs,
        out_shape=out_shape,
        scratch_shapes=[pltpu.VMEM((TOKEN_TILE, D_MODEL), BF16)],
        compiler_params=_cparams(1),
        name="project_latent" if with_rope else "project_context",
    )(*args)


def _lane_mask(width, lo, size, dtype):
    lane = lax.broadcasted_iota(jnp.int32, (1, width), 1)
    return ((lane >= lo) & (lane < lo + size)).astype(dtype)


def _fold_lanes(x, op):
    r = x[:, :LANES]
    for j in range(1, x.shape[1] // LANES):
        r = op(r, x[:, j * LANES:(j + 1) * LANES])
    return r


def _diff_lambda(lp_ref, lam_init):
    lp = lp_ref[...]
    s1 = jnp.sum(lp[0:1, :] * lp[1:2, :], axis=-1, keepdims=True)
    s2 = jnp.sum(lp[2:3, :] * lp[3:4, :], axis=-1, keepdims=True)
    return jnp.exp(s1) - jnp.exp(s2) + lam_init


def _subln_accumulate(o_a, o_full, head, tq):
    mask = _lane_mask(256, head * A_DV, A_DV, F32)
    om = o_full * mask
    ms = jnp.sum(om * om, axis=-1, keepdims=True) * (1.0 / A_DV)
    return o_a + om * lax.rsqrt(ms + RMS_EPS)


def _finish(o_parts, x_ref, gt_ref, wo_ref, g_ref, b_ref, o_ref):
    o = jnp.concatenate(o_parts, axis=1).astype(BF16)
    f = _dot(o, wo_ref[...])
    y = DN_ALPHA * x_ref[...] + gt_ref[...] * f
    o_ref[...] = _layernorm(y, g_ref[...], b_ref[...])


def _attn_ctx_kernel(qa_ref, ka_ref, va_ref, qb_ref, kb_ref, vb_ref, qc_ref, kc_ref,
                     x_ref, gt_ref, wo_ref, g_ref, b_ref, lp_ref, sg_ref, sink_ref, wuv_ref,
                     o_ref, *, lam_init):
    tq = qa_ref.shape[0]
    lam = _diff_lambda(lp_ref, lam_init)

    def softmax_parts(s):
        m = jnp.max(s, axis=-1, keepdims=True)
        e = jnp.exp2(s - m)
        return e, jnp.sum(e, axis=-1, keepdims=True)

    qa = qa_ref[...]
    ka = ka_ref[...]
    va = va_ref[...]
    o_a = jnp.zeros((tq, 256), F32)
    for h in range(A_HEADS):
        q2 = jnp.concatenate([qa * _lane_mask(256, h * 64 + j * A_DQK, A_DQK, BF16) for j in range(2)], axis=0)
        e, l = softmax_parts(_dot_nt(q2, ka))
        acc = _dot(e.astype(BF16), va) / l
        o_a = _subln_accumulate(o_a, acc[:tq] - lam * acc[tq:], h, tq)
    o_a = o_a * sg_ref[...]

    qb = qb_ref[...]
    kb = kb_ref[...]
    vb = vb_ref[...]
    o_b = []
    for c in range(B_KV_HEADS):
        cs = slice(256 * c, 256 * c + 256)
        masks = [_lane_mask(256, g * B_DH, B_DH, BF16) for g in range(B_GROUP)]
        q4 = jnp.concatenate([qb[:, cs] * mk for mk in masks], axis=0)
        sink = jnp.concatenate([jnp.full((tq, 1), sink_ref[c * B_GROUP + g] * LOG2E, F32)
                                for g in range(B_GROUP)], axis=0)
        s = _dot_nt(q4, kb[:, cs])
        m = jnp.maximum(jnp.max(s, axis=-1, keepdims=True), sink)
        e = jnp.exp2(s - m)
        den = jnp.sum(e, axis=-1, keepdims=True) + jnp.exp2(sink - m)
        acc = _dot(e.astype(BF16), vb[:, cs]) / den
        oc = acc[:tq] * masks[0].astype(F32)
        for g in range(1, B_GROUP):
            oc = oc + acc[g * tq:(g + 1) * tq] * masks[g].astype(F32)
        o_b.append(oc)

    qc = qc_ref[...]
    kc = kc_ref[...]
    q4 = jnp.concatenate([qc[:, 256 * h:256 * h + 256] for h in range(C_HEADS)], axis=0)
    e, l = softmax_parts(_dot_nt(q4, kc))
    acc = _dot(e.astype(BF16), kc) / l
    o_lat = jnp.concatenate([acc[h * tq:(h + 1) * tq, :C_KV_RANK] for h in range(C_HEADS)], axis=1)
    o_c = _dot(o_lat.astype(BF16), wuv_ref[...])

    _finish([o_a] + o_b + [o_c], x_ref, gt_ref, wo_ref, g_ref, b_ref, o_ref)


def _attn_ctx(proj, x, mod_l, wo, g, b, lp, sg, sink, wuv, lam_init, seq):
    qa, ka, va, qb, kb, vb, qc, kc = proj
    n = x.shape[0]

    def rows(width):
        return pl.BlockSpec((seq, width), lambda i: (i, 0))

    return pl.pallas_call(
        functools.partial(_attn_ctx_kernel, lam_init=lam_init),
        grid=(n // seq,),
        in_specs=[rows(256), rows(256), rows(256), rows(512), rows(512), rows(512), rows(1024), rows(256),
                  rows(D_MODEL), _mod_spec(5, lambda i: 0),
                  _resident(wo.shape), _resident(g.shape), _resident(b.shape), _resident(lp.shape),
                  _resident(sg.shape), pl.BlockSpec(memory_space=pltpu.SMEM), _resident(wuv.shape)],
        out_specs=rows(D_MODEL),
        out_shape=jax.ShapeDtypeStruct((n, D_MODEL), F32),
        compiler_params=_cparams(1),
        name="attention_context",
    )(qa, ka, va, qb, kb, vb, qc, kc, x, mod_l, wo, g, b, lp, sg, sink, wuv)


def _attn_lat_kernel(qa_ref, qb_ref, qc_ref,
                     ka_ref, va_ref, kb_ref, vb_ref, kc_ref,
                     xka_ref, xva_ref, xkb_ref, xvb_ref, xkc_ref,
                     x_ref, gt_ref, wo_ref, g_ref, b_ref, lp_ref, sg_ref, sink_ref, wuv_ref,
                     o_ref, q_scr, s_scr, *, lam_init, n_lat):
    tq = qa_ref.shape[0]
    n_ctx = xka_ref.shape[0]
    n_chunks = n_lat // KEY_CHUNK
    lam = _diff_lambda(lp_ref, lam_init)

    def full_softmax_pv(rows, k_ref, xk_ref, v_ref, xv_ref):
        def key_span(c):
            return slice(n_ctx + c * KEY_CHUNK, n_ctx + (c + 1) * KEY_CHUNK)

        def key_rows(c):
            return slice(c * KEY_CHUNK, (c + 1) * KEY_CHUNK)

        s = _dot_nt(q_scr[:rows, :], xk_ref[...])
        s_scr[:rows, :n_ctx] = s
        m_run = _fold_lanes(s, jnp.maximum)
        for c in range(n_chunks):
            s = _dot_nt(q_scr[:rows, :], k_ref[key_rows(c), :])
            s_scr[:rows, key_span(c)] = s
            m_run = jnp.maximum(m_run, _fold_lanes(s, jnp.maximum))
        m = jnp.max(m_run, axis=-1, keepdims=True)

        e = jnp.exp2(s_scr[:rows, :n_ctx] - m)
        l_run = _fold_lanes(e, jnp.add)
        acc = _dot(e.astype(BF16), xv_ref[...])
        for c in range(n_chunks):
            e = jnp.exp2(s_scr[:rows, key_span(c)] - m)
            l_run = l_run + _fold_lanes(e, jnp.add)
            acc = acc + _dot(e.astype(BF16), v_ref[key_rows(c), :])
        return acc, jnp.sum(l_run, axis=-1, keepdims=True)

    qa = qa_ref[...]
    for hj in range(2 * A_HEADS):
        q_scr[hj * tq:(hj + 1) * tq, :] = qa * _lane_mask(256, hj * A_DQK, A_DQK, BF16)
    acc, l = full_softmax_pv(2 * A_HEADS * tq, ka_ref, xka_ref, va_ref, xva_ref)
    acc = acc / l
    o_a = jnp.zeros((tq, 256), F32)
    for h in range(A_HEADS):
        o_h = acc[2 * h * tq:(2 * h + 1) * tq] - lam * acc[(2 * h + 1) * tq:(2 * h + 2) * tq]
        o_a = _subln_accumulate(o_a, o_h, h, tq)
    o_a = o_a * sg_ref[...]

    i = pl.program_id(1)
    q0 = i * tq
    win = tq + 2 * WINDOW
    start = pl.multiple_of(jnp.clip(q0 - WINDOW, 0, n_lat - win), BLOCK)
    kpos = start + lax.broadcasted_iota(jnp.int32, (1, win), 1)
    qpos = q0 + (lax.broadcasted_iota(jnp.int32, (B_GROUP * tq, 1), 0) & (tq - 1))
    valid = jnp.abs(kpos - qpos) <= WINDOW
    qb = qb_ref[...]
    o_b = []
    for c in range(B_KV_HEADS):
        cs = slice(256 * c, 256 * c + 256)
        masks = [_lane_mask(256, g * B_DH, B_DH, BF16) for g in range(B_GROUP)]
        q4 = jnp.concatenate([qb[:, cs] * mk for mk in masks], axis=0)
        sink = jnp.concatenate([jnp.full((tq, 1), sink_ref[c * B_GROUP + g] * LOG2E, F32)
                                for g in range(B_GROUP)], axis=0)
        s_ctx = _dot_nt(q4, xkb_ref[:, cs])
        s_loc = jnp.where(valid, _dot_nt(q4, kb_ref[pl.ds(start, win), cs]), NEG_BIG)
        m = jnp.maximum(jnp.maximum(jnp.max(s_ctx, axis=-1, keepdims=True),
                                    jnp.max(s_loc, axis=-1, keepdims=True)), sink)
        e_ctx = jnp.exp2(s_ctx - m)
        e_loc = jnp.exp2(s_loc - m)
        den = (jnp.sum(e_ctx, axis=-1, keepdims=True) + jnp.sum(e_loc, axis=-1, keepdims=True)
               + jnp.exp2(sink - m))
        acc = (_dot(e_ctx.astype(BF16), xvb_ref[:, cs])
               + _dot(e_loc.astype(BF16), vb_ref[pl.ds(start, win), cs])) / den
        oc = acc[:tq] * masks[0].astype(F32)
        for g in range(1, B_GROUP):
            oc = oc + acc[g * tq:(g + 1) * tq] * masks[g].astype(F32)
        o_b.append(oc)

    for h in range(C_HEADS):
        q_scr[h * tq:(h + 1) * tq, :] = qc_ref[:, 256 * h:256 * (h + 1)]
    acc, l = full_softmax_pv(C_HEADS * tq, kc_ref, xkc_ref, kc_ref, xkc_ref)
    acc = acc / l
    o_lat = jnp.concatenate([acc[h * tq:(h + 1) * tq, :C_KV_RANK] for h in range(C_HEADS)], axis=1)
    o_c = _dot(o_lat.astype(BF16), wuv_ref[...])

    _finish([o_a] + o_b + [o_c], x_ref, gt_ref, wo_ref, g_ref, b_ref, o_ref)


def _attn_lat(proj, ctx, x, mod_l, wo, g, b, lp, sg, sink, wuv, lam_init, n_batch, n_lat):
    qa, ka, va, qb, kb, vb, qc, kc = proj
    xka, xva, xkb, xvb, xkc = ctx
    tq = BLOCK
    tiles = n_lat // tq
    n_ctx = xka.shape[1]

    def qrows(width):
        return pl.BlockSpec((tq, width), lambda bi, i: (bi * tiles + i, 0))

    def seq(width):
        return pl.BlockSpec((n_lat, width), lambda bi, i: (bi, 0), pipeline_mode=pl.Buffered(1))

    def cseq(width):
        return pl.BlockSpec((None, n_ctx, width), lambda bi, i: (bi, 0, 0))

    def res(a):
        nd = a.ndim
        return pl.BlockSpec(a.shape, lambda bi, i: (0,) * nd, pipeline_mode=pl.Buffered(1))

    return pl.pallas_call(
        functools.partial(_attn_lat_kernel, lam_init=lam_init, n_lat=n_lat),
        grid=(n_batch, tiles),
        in_specs=[qrows(256), qrows(512), qrows(1024),
                  seq(256), seq(256), seq(512), seq(512), seq(256),
                  cseq(256), cseq(256), cseq(512), cseq(512), cseq(256),
                  qrows(D_MODEL),
                  pl.BlockSpec((None, None, 1, D_MODEL), lambda bi, i: (1 + bi, 5, 0, 0)),
                  res(wo), res(g), res(b), res(lp), res(sg),
                  pl.BlockSpec(memory_space=pltpu.SMEM), res(wuv)],
        out_specs=qrows(D_MODEL),
        out_shape=jax.ShapeDtypeStruct((n_batch * n_lat, D_MODEL), F32),
        scratch_shapes=[pltpu.VMEM((2 * A_HEADS * tq, 256), BF16),
                        pltpu.VMEM((2 * A_HEADS * tq, n_ctx + n_lat), F32)],
        compiler_params=_cparams(2),
        name="attention_latent",
    )(qa, qb, qc, ka, va, kb, vb, kc, xka, xva, xkb, xvb, xkc, x, mod_l, wo, g, b, lp, sg, sink, wuv)


def _rope_tables(rows, dim):
    row = jnp.repeat(jnp.arange(rows, dtype=F32), GRID_W)
    col = jnp.tile(jnp.arange(GRID_W, dtype=F32), rows)
    a = dim // 2
    inv = jnp.power(ROPE_BASE, -jnp.arange(0, a, 2, dtype=F32) / a)
    ar = row[:, None] * inv[None, :]
    ac = col[:, None] * inv[None, :]
    ang = jnp.concatenate([ar, ar, ac, ac], axis=-1)
    reps = LANES // dim
    cos = jnp.tile(jnp.cos(ang), (1, reps))
    sin = jnp.tile(jnp.sin(ang), (1, reps))
    lower = (np.arange(LANES) % (dim // 2)) < (dim // 4)
    sin_lo = jnp.where(lower[None, :], -sin, 0.0)
    sin_hi = jnp.where(lower[None, :], 0.0, sin)
    return cos, sin_lo, sin_hi


def _extend_w_in(w_in_l):
    o = np.cumsum([0, 256, 256, 256, 512, 128, 128, 256, 128, 32])
    a_q, a_k, a_v, b_q, b_k, b_v, c_qd, c_kvd, c_kpe = [w_in_l[:, o[i]:o[i + 1]] for i in range(9)]

    def rep(w):
        return jnp.concatenate([w[:, B_DH * c:B_DH * (c + 1)] for c in range(B_KV_HEADS) for _ in range(B_GROUP)], axis=1)

    pad = jnp.zeros((D_MODEL, W_EXT - OFF_CKPE - C_ROPE), w_in_l.dtype)
    return jnp.concatenate([a_q, a_k, a_v, b_q, rep(b_k), rep(b_v), c_qd, c_kvd, c_kpe, pad], axis=1).astype(BF16)


def _c_weights(w_q_up_l, w_kv_up_l):
    wq = w_q_up_l.reshape(C_Q_RANK, C_HEADS, C_NOPE + C_ROPE)
    wqu = jnp.concatenate([wq[:, :, :C_NOPE].reshape(C_Q_RANK, -1), wq[:, :, C_NOPE:].reshape(C_Q_RANK, -1)], axis=1)
    w_uk = w_kv_up_l[:, :, :C_NOPE]
    w_uv = w_kv_up_l[:, :, C_NOPE:]
    wc2 = jnp.zeros((C_HEADS * C_NOPE + C_HEADS * C_ROPE, C_HEADS * 256), F32)
    wuv = jnp.zeros((C_HEADS * C_KV_RANK, C_HEADS * C_DV), F32)
    eye = jnp.eye(C_ROPE, dtype=F32)
    for h in range(C_HEADS):
        wc2 = wc2.at[C_NOPE * h:C_NOPE * (h + 1), 256 * h:256 * h + C_KV_RANK].set(w_uk[:, h, :].T)
        r0 = C_HEADS * C_NOPE + C_ROPE * h
        wc2 = wc2.at[r0:r0 + C_ROPE, 256 * h + C_KV_RANK:256 * h + C_KV_RANK + C_ROPE].set(eye)
        wuv = wuv.at[C_KV_RANK * h:C_KV_RANK * (h + 1), C_DV * h:C_DV * (h + 1)].set(w_uv[:, h, :])
    return wqu.astype(BF16), wc2.astype(BF16), wuv.astype(BF16)


def _ffn_weights(w1, w3, w2):
    w1c = w1.reshape(D_MODEL, N_FF_CHUNKS, FF_CHUNK).transpose(1, 0, 2).astype(BF16)
    w3c = w3.reshape(D_MODEL, N_FF_CHUNKS, FF_CHUNK).transpose(1, 0, 2).astype(BF16)
    w2c = w2.reshape(N_FF_CHUNKS, FF_CHUNK, D_MODEL).astype(BF16)
    return w1c, w3c, w2c


def _rep_kv(t):
    b, n = t.shape[:2]
    return jnp.broadcast_to(t[:, :, :, None, :], (b, n, B_KV_HEADS, B_GROUP, B_DH)).reshape(b, n, -1)


def kernel(x_prompt, x_sample, cache_a_k, cache_a_v, cache_b_k, cache_b_v, cache_c_kv, cache_c_kpe, c, c_ctx,
           w_mod, b_mod, ln_g, ln_b, ffn_w1, ffn_w3, ffn_w2, w_in, w_o, a_lambda, a_subln_g, b_sink,
           c_q_norm_g, c_w_q_up, c_kv_norm_g, c_w_kv_up):
    n_req, seq, _ = x_prompt.shape
    n_dec, n_lat, _ = x_sample.shape
    n_ctx = cache_a_k.shape[2]
    assert n_lat % KEY_CHUNK == 0 and n_lat % TOKEN_TILE == 0 and (n_req * seq) % TOKEN_TILE == 0
    assert 1 + n_dec <= MOD_ROWS

    cond8 = jnp.concatenate([c_ctx[None, :], c, jnp.zeros((MOD_ROWS - 1 - n_dec, D_MODEL), F32)], axis=0)
    mod = _modulation(cond8, w_mod, b_mod[:, None, :]).reshape(DEPTH, MOD_ROWS, N_MOD, 1, D_MODEL)

    tabs = _rope_tables(n_lat // GRID_W, A_DQK) + _rope_tables(n_lat // GRID_W, B_DH)
    tiles_per_lat = n_lat // TOKEN_TILE

    def prompt_row(i):
        return 0

    def sample_row(i):
        return 1 + i // tiles_per_lat

    xp = x_prompt.reshape(n_req * seq, D_MODEL)
    xs = x_sample.reshape(n_dec * n_lat, D_MODEL)
    states = []
    for l in range(DEPTH):
        lam_init = 0.8 - 0.6 * math.exp(-0.3 * l)
        mod_l = mod[l]
        ffn1 = _ffn_weights(ffn_w1[l, 0], ffn_w3[l, 0], ffn_w2[l, 0])
        ffn2 = _ffn_weights(ffn_w1[l, 1], ffn_w3[l, 1], ffn_w2[l, 1])
        g0, g1, g2 = (ln_g[l, k][None, :] for k in range(3))
        b0, b1, b2 = (ln_b[l, k][None, :] for k in range(3))
        w_in_ext = _extend_w_in(w_in[l])
        wqu, wc2, wuv = _c_weights(c_w_q_up[l], c_w_kv_up[l])
        qg = c_q_norm_g[l][None, :]
        kvg = c_kv_norm_g[l][None, :]
        wo = w_o[l].astype(BF16)
        sg = jnp.tile(a_subln_g[l], A_HEADS)[None, :] * (1.0 - lam_init)
        attn_w = (wo, g1, b1, a_lambda[l], sg, b_sink[l], wuv)

        xp = _ffn(xp, mod_l, 0, *ffn1, g0, b0, prompt_row)
        outs = _project(xp, mod_l, w_in_ext, qg, wqu, wc2, kvg, prompt_row, with_state=True)
        states.append(outs[8:])
        xp = _attn_ctx(outs[:8], xp, mod_l, *attn_w, lam_init, seq)
        xp = _ffn(xp, mod_l, 2, *ffn2, g2, b2, prompt_row)

        xs = _ffn(xs, mod_l, 0, *ffn1, g0, b0, sample_row)
        outs = _project(xs, mod_l, w_in_ext, qg, wqu, wc2, kvg, sample_row, tabs=tabs)
        zeros = jnp.zeros((n_dec, n_ctx, 256 - C_KV_RANK - C_ROPE), F32)
        ctx = (cache_a_k[:, l].reshape(n_dec, n_ctx, -1), cache_a_v[:, l].reshape(n_dec, n_ctx, -1),
               _rep_kv(cache_b_k[:, l]), _rep_kv(cache_b_v[:, l]),
               jnp.concatenate([cache_c_kv[:, l], cache_c_kpe[:, l], zeros], axis=-1))
        ctx = tuple(t.astype(BF16) for t in ctx)
        xs = _attn_lat(outs, ctx, xs, mod_l, *attn_w, lam_init, n_dec, n_lat)
        xs = _ffn(xs, mod_l, 2, *ffn2, g2, b2, sample_row)

    def stacked(k, tail):
        return jnp.stack([states[l][k].reshape((n_req, seq) + tail) for l in range(DEPTH)], axis=1)

    return (xp.reshape(n_req, seq, D_MODEL), xs.reshape(n_dec, n_lat, D_MODEL),
            stacked(0, (A_HEADS, 2 * A_DQK)), stacked(1, (A_HEADS, A_DV)),
            stacked(2, (B_KV_HEADS, B_DH)), stacked(3, (B_KV_HEADS, B_DH)),
            stacked(4, (C_KV_RANK,)), stacked(5, (C_ROPE,)))
```

```python
import functools
import math

import jax
import jax.numpy as jnp
import numpy as np
from jax import lax
from jax.experimental import pallas as pl
from jax.experimental.pallas import tpu as pltpu

D_MODEL = 1024
DEPTH = 2
GRID_W = 64
BLOCK = 128
A_HEADS = 4
A_DQK = 32
A_DV = 64
B_HEADS = 8
B_KV_HEADS = 2
B_GROUP = B_HEADS // B_KV_HEADS
B_DH = 64
WINDOW = 128
C_HEADS = 4
C_Q_RANK = 256
C_KV_RANK = 128
C_NOPE = 64
C_ROPE = 32
C_DV = 64
D_FF = 2816
N_MOD = 9
ROPE_BASE = 10000.0
LN_EPS = 1e-5
RMS_EPS = 1e-6
DN_ALPHA = (2 * DEPTH) ** 0.25
FFN_RES = 0.5
LOG2E = 1.4426950408889634

LANES = 128
MXU_DIM = 256
FF_CHUNK = MXU_DIM
N_FF_CHUNKS = D_FF // FF_CHUNK
TOKEN_TILE = 512
MOD_ROWS = 8
MOD_COL_TILE = 1024
KEY_CHUNK = 512
VMEM_LIMIT = 56 * 1024 * 1024
NEG_BIG = -1e30

OFF_AQ = 0
OFF_AK = 256
OFF_AV = 512
OFF_BQ = 768
OFF_BK = 1280
OFF_BV = 1408
OFF_CQD = 1536
OFF_CKVD = 1792
OFF_CKPE = 1920
IN_WIDTH = 1952

BF16 = jnp.bfloat16
F32 = jnp.float32


def _dot(a, b):
    return jnp.dot(a, b, preferred_element_type=F32)


def _dot_nt(a, b):
    return lax.dot_general(a, b, (((1,), (1,)), ((), ())), preferred_element_type=F32)


def _cparams(n_grid):
    return pltpu.CompilerParams(dimension_semantics=("arbitrary",) * n_grid, vmem_limit_bytes=VMEM_LIMIT)


def _pick(arr, *lead):
    tail = arr.shape[len(lead):]
    zeros = (0,) * len(tail)
    return pl.BlockSpec((None,) * len(lead) + tail, lambda *_: lead + zeros, pipeline_mode=pl.Buffered(1))


def _mod_spec(layer, slot, row_fn):
    return pl.BlockSpec((None, None, None, 1, D_MODEL), lambda i, *_: (layer, row_fn(i), slot, 0, 0))


def _layernorm(y, g, b):
    mu = jnp.mean(y, axis=-1, keepdims=True)
    d = y - mu
    var = jnp.mean(d * d, axis=-1, keepdims=True)
    return d * lax.rsqrt(var + LN_EPS) * g + b


def _mod_kernel(c_ref, w_ref, b_ref, o_ref):
    c = c_ref[...]
    s = (c / (1.0 + jnp.exp(-c))).astype(BF16)
    o_ref[...] = _dot(s, w_ref[...].astype(BF16)) + b_ref[...]


def _modulation(cond8, w_mod, b_mod):
    n_col = N_MOD * D_MODEL
    return pl.pallas_call(
        _mod_kernel,
        grid=(DEPTH, n_col // MOD_COL_TILE),
        in_specs=[
            pl.BlockSpec((MOD_ROWS, D_MODEL), lambda l, j: (0, 0)),
            pl.BlockSpec((None, D_MODEL, MOD_COL_TILE), lambda l, j: (l, 0, j)),
            pl.BlockSpec((None, 1, MOD_COL_TILE), lambda l, j: (l, 0, j)),
        ],
        out_specs=pl.BlockSpec((None, MOD_ROWS, MOD_COL_TILE), lambda l, j: (l, 0, j)),
        out_shape=jax.ShapeDtypeStruct((DEPTH, MOD_ROWS, n_col), F32),
        compiler_params=_cparams(2),
        name="modulation",
    )(cond8, w_mod, b_mod)


def _ffn_kernel(x_ref, sh_ref, sc_ref, gt_ref, w1_ref, w3_ref, w2_ref, g_ref, b_ref, o_ref, h_scr):
    x = x_ref[...]
    h_scr[...] = (x * (1.0 + sc_ref[...]) + sh_ref[...]).astype(BF16)
    acc = None
    for c in range(N_FF_CHUNKS):
        cs = slice(c * FF_CHUNK, (c + 1) * FF_CHUNK)
        h = h_scr[...]
        a = _dot(h, w1_ref[:, cs])
        b = _dot(h, w3_ref[:, cs])
        u = (a / (1.0 + jnp.exp(-a))) * b
        d = _dot(u.astype(BF16), w2_ref[cs, :])
        acc = d if acc is None else acc + d
    y = DN_ALPHA * x + FFN_RES * gt_ref[...] * acc
    o_ref[...] = _layernorm(y, g_ref[...], b_ref[...])


def _ffn(x, mod, layer, half, w1, w3, w2, ln_g, ln_b, row_fn):
    n = x.shape[0]
    slot = 2 * half
    tile = pl.BlockSpec((TOKEN_TILE, D_MODEL), lambda i: (i, 0))
    return pl.pallas_call(
        _ffn_kernel,
        grid=(n // TOKEN_TILE,),
        in_specs=[
            tile,
            _mod_spec(layer, 3 * slot, row_fn), _mod_spec(layer, 3 * slot + 1, row_fn),
            _mod_spec(layer, 3 * slot + 2, row_fn),
            _pick(w1, layer, half), _pick(w3, layer, half), _pick(w2, layer, half),
            _pick(ln_g, layer, slot), _pick(ln_b, layer, slot),
        ],
        out_specs=tile,
        out_shape=jax.ShapeDtypeStruct((n, D_MODEL), F32),
        scratch_shapes=[pltpu.VMEM((TOKEN_TILE, D_MODEL), BF16)],
        compiler_params=_cparams(1),
        name="ffn_half",
    )(x, mod, mod, mod, w1, w3, w2, ln_g, ln_b)


def _rope(x, cos, sin_lo, sin_hi, q):
    return x * cos + pltpu.roll(x, LANES - q, 1) * sin_lo + pltpu.roll(x, q, 1) * sin_hi


def _rope_cols(x, tabs, q):
    cos, sin_lo, sin_hi = tabs
    parts = [_rope(x[:, j * LANES:(j + 1) * LANES], cos, sin_lo, sin_hi, q) for j in range(x.shape[1] // LANES)]
    return parts[0] if len(parts) == 1 else jnp.concatenate(parts, axis=1)


def _rms(x, g):
    return x * lax.rsqrt(jnp.mean(x * x, axis=-1, keepdims=True) + RMS_EPS) * g


def _repeat_kv_heads(t):
    lane = lax.broadcasted_iota(jnp.int32, (1, LANES), 1)
    first = lane < B_DH
    swapped = pltpu.roll(t, B_DH, 1)
    k0 = jnp.where(first, t, swapped)
    k1 = jnp.where(first, swapped, t)
    return jnp.concatenate([k0, k0, k1, k1], axis=1)


def _project_kernel(*refs, with_rope, with_state):
    (x_ref, sh_ref, sc_ref, win_ref, wkpe_ref, qg_ref, wqu_ref, wc2_ref, kvg_ref), rest = refs[:9], refs[9:]
    if with_rope:
        tab_refs, rest = rest[:6], rest[6:]
    (qa_ref, ka_ref, va_ref, qb_ref, kb_ref, vb_ref, qc_ref, kc_ref), rest = rest[:8], rest[8:]
    h_scr = rest[-1]
    rest = rest[:-1]

    h_scr[...] = (x_ref[...] * (1.0 + sc_ref[...]) + sh_ref[...]).astype(BF16)

    def piece(off, width):
        return _dot(h_scr[...], win_ref[:, off:off + width])

    a_q = piece(OFF_AQ, 256)
    a_k = piece(OFF_AK, 256)
    a_v = piece(OFF_AV, 256)
    b_q = piece(OFF_BQ, 512)
    b_k = piece(OFF_BK, 128)
    b_v = piece(OFF_BV, 128)
    c_qd = piece(OFF_CQD, 256)
    c_kvd = piece(OFF_CKVD, 128)
    c_kpe = _dot(h_scr[...], wkpe_ref[...])

    c_q = _dot(_rms(c_qd, qg_ref[...]).astype(BF16), wqu_ref[...])
    c_qn = c_q[:, :256]
    c_qp = c_q[:, 256:384]
    c_kv = _rms(c_kvd, kvg_ref[...])

    if with_state:
        sak_ref, sav_ref, sbk_ref, sbv_ref, sckv_ref, sckpe_ref = rest
        sak_ref[...] = a_k
        sav_ref[...] = a_v
        sbk_ref[...] = b_k
        sbv_ref[...] = b_v
        sckv_ref[...] = c_kv
        sckpe_ref[...] = c_kpe[:, :C_ROPE]

    if with_rope:
        ta = tuple(r[...] for r in tab_refs[:3])
        tb = tuple(r[...] for r in tab_refs[3:])
        lane = lax.broadcasted_iota(jnp.int32, (1, LANES), 1)
        keep = (lane < C_ROPE).astype(F32)
        tck = tuple(v * keep for v in ta)
        a_q = _rope_cols(a_q, ta, A_DQK // 4)
        a_k = _rope_cols(a_k, ta, A_DQK // 4)
        b_q = _rope_cols(b_q, tb, B_DH // 4)
        b_k = _rope_cols(b_k, tb, B_DH // 4)
        c_qp = _rope_cols(c_qp, ta, C_ROPE // 4)
        c_kpe = _rope_cols(c_kpe, tck, C_ROPE // 4)

    qa_ref[...] = (a_q * (A_DQK ** -0.5 * LOG2E)).astype(BF16)
    ka_ref[...] = a_k.astype(BF16)
    va_ref[...] = a_v.astype(BF16)
    qb_ref[...] = (b_q * (B_DH ** -0.5 * LOG2E)).astype(BF16)
    kb_ref[...] = _repeat_kv_heads(b_k).astype(BF16)
    vb_ref[...] = _repeat_kv_heads(b_v).astype(BF16)
    c_scale = (C_NOPE + C_ROPE) ** -0.5 * LOG2E
    c_qcat = jnp.concatenate([c_qn * c_scale, c_qp * c_scale], axis=1).astype(BF16)
    qc_ref[...] = _dot(c_qcat, wc2_ref[...]).astype(BF16)
    kc_ref[...] = jnp.concatenate([c_kv, c_kpe], axis=1).astype(BF16)


def _project(x, mod, layer, w_in, w_kpe, qg, wqu, wc2, kvg, row_fn, tabs=None, with_state=False):
    n = x.shape[0]
    with_rope = tabs is not None

    def rows(width):
        return pl.BlockSpec((TOKEN_TILE, width), lambda i: (i, 0))

    in_specs = [rows(D_MODEL), _mod_spec(layer, 3, row_fn), _mod_spec(layer, 4, row_fn),
                _pick(w_in, layer), _pick(w_kpe, layer), _pick(qg, layer), _pick(wqu, layer),
                _pick(wc2, layer), _pick(kvg, layer)]
    args = [x, mod, mod, w_in, w_kpe, qg, wqu, wc2, kvg]
    if with_rope:
        tiles_per_seq = tabs[0].shape[0] // TOKEN_TILE
        in_specs += [pl.BlockSpec((TOKEN_TILE, LANES), lambda i: (i % tiles_per_seq, 0))] * 6
        args += list(tabs)
    widths = [256, 256, 256, 512, 512, 512, 1024, 256]
    out_specs = [rows(w) for w in widths]
    out_shape = [jax.ShapeDtypeStruct((n, w), BF16) for w in widths]
    if with_state:
        swidths = [256, 256, 128, 128, 128, C_ROPE]
        out_specs += [rows(w) for w in swidths]
        out_shape += [jax.ShapeDtypeStruct((n, w), F32) for w in swidths]
    return pl.pallas_call(
        functools.partial(_project_kernel, with_rope=with_rope, with_state=with_state),
        grid=(n // TOKEN_TILE,),
        in_specs=in_specs,
        out_specs=out_specs,
        out_shape=out_shape,
        scratch_shapes=[pltpu.VMEM((TOKEN_TILE, D_MODEL), BF16)],
        compiler_params=_cparams(1),
        name="project_latent" if with_rope else "project_context",
    )(*args)


def _lane_mask(width, lo, size, dtype):
    lane = lax.broadcasted_iota(jnp.int32, (1, width), 1)
    return ((lane >= lo) & (lane < lo + size)).astype(dtype)


def _fold_lanes(x, op):
    r = x[:, :LANES]
    for j in range(1, x.shape[1] // LANES):
        r = op(r, x[:, j * LANES:(j + 1) * LANES])
    return r


def _diff_lambda(lp_ref, lam_init):
    lp = lp_ref[...]
    s1 = jnp.sum(lp[0:1, :] * lp[1:2, :], axis=-1, keepdims=True)
    s2 = jnp.sum(lp[2:3, :] * lp[3:4, :], axis=-1, keepdims=True)
    return jnp.exp(s1) - jnp.exp(s2) + lam_init


def _subln_accumulate(o_a, o_full, head, tq):
    mask = _lane_mask(256, head * A_DV, A_DV, F32)
    om = o_full * mask
    ms = jnp.sum(om * om, axis=-1, keepdims=True) * (1.0 / A_DV)
    return o_a + om * lax.rsqrt(ms + RMS_EPS)


def _finish(o_parts, x_ref, gt_ref, wo_ref, g_ref, b_ref, o_ref):
    o = jnp.concatenate(o_parts, axis=1).astype(BF16)
    f = _dot(o, wo_ref[...])
    y = DN_ALPHA * x_ref[...] + gt_ref[...] * f
    o_ref[...] = _layernorm(y, g_ref[...], b_ref[...])


def _attn_ctx_kernel(qa_ref, ka_ref, va_ref, qb_ref, kb_ref, vb_ref, qc_ref, kc_ref,
                     x_ref, gt_ref, wo_ref, g_ref, b_ref, lp_ref, sg_ref, sink_ref, wuv_ref,
                     o_ref, *, lam_init, layer):
    tq = qa_ref.shape[0]
    lam = _diff_lambda(lp_ref, lam_init)

    def softmax_parts(s):
        m = jnp.max(s, axis=-1, keepdims=True)
        e = jnp.exp2(s - m)
        return e, jnp.sum(e, axis=-1, keepdims=True)

    qa = qa_ref[...]
    ka = ka_ref[...]
    va = va_ref[...]
    o_a = jnp.zeros((tq, 256), F32)
    for h in range(A_HEADS):
        q2 = jnp.concatenate([qa * _lane_mask(256, h * 64 + j * A_DQK, A_DQK, BF16) for j in range(2)], axis=0)
        e, l = softmax_parts(_dot_nt(q2, ka))
        acc = _dot(e.astype(BF16), va) / l
        o_a = _subln_accumulate(o_a, acc[:tq] - lam * acc[tq:], h, tq)
    o_a = o_a * sg_ref[...]

    qb = qb_ref[...]
    kb = kb_ref[...]
    vb = vb_ref[...]
    o_b = []
    for c in range(B_KV_HEADS):
        cs = slice(256 * c, 256 * c + 256)
        masks = [_lane_mask(256, g * B_DH, B_DH, BF16) for g in range(B_GROUP)]
        q4 = jnp.concatenate([qb[:, cs] * mk for mk in masks], axis=0)
        sink = jnp.concatenate([jnp.full((tq, 1), sink_ref[layer, c * B_GROUP + g] * LOG2E, F32)
                                for g in range(B_GROUP)], axis=0)
        s = _dot_nt(q4, kb[:, cs])
        m = jnp.maximum(jnp.max(s, axis=-1, keepdims=True), sink)
        e = jnp.exp2(s - m)
        den = jnp.sum(e, axis=-1, keepdims=True) + jnp.exp2(sink - m)
        acc = _dot(e.astype(BF16), vb[:, cs]) / den
        oc = acc[:tq] * masks[0].astype(F32)
        for g in range(1, B_GROUP):
            oc = oc + acc[g * tq:(g + 1) * tq] * masks[g].astype(F32)
        o_b.append(oc)

    qc = qc_ref[...]
    kc = kc_ref[...]
    q4 = jnp.concatenate([qc[:, 256 * h:256 * h + 256] for h in range(C_HEADS)], axis=0)
    e, l = softmax_parts(_dot_nt(q4, kc))
    acc = _dot(e.astype(BF16), kc) / l
    o_lat = jnp.concatenate([acc[h * tq:(h + 1) * tq, :C_KV_RANK] for h in range(C_HEADS)], axis=1)
    o_c = _dot(o_lat.astype(BF16), wuv_ref[...])

    _finish([o_a] + o_b + [o_c], x_ref, gt_ref, wo_ref, g_ref, b_ref, o_ref)


def _attn_ctx(proj, x, mod, layer, wo, ln_g, ln_b, lp, sg, sink, wuv, lam_init, seq):
    qa, ka, va, qb, kb, vb, qc, kc = proj
    n = x.shape[0]

    def rows(width):
        return pl.BlockSpec((seq, width), lambda i: (i, 0))

    return pl.pallas_call(
        functools.partial(_attn_ctx_kernel, lam_init=lam_init, layer=layer),
        grid=(n // seq,),
        in_specs=[rows(256), rows(256), rows(256), rows(512), rows(512), rows(512), rows(1024), rows(256),
                  rows(D_MODEL), _mod_spec(layer, 5, lambda i: 0),
                  _pick(wo, layer), _pick(ln_g, layer, 1), _pick(ln_b, layer, 1), _pick(lp, layer),
                  _pick(sg, layer), pl.BlockSpec(memory_space=pltpu.SMEM), _pick(wuv, layer)],
        out_specs=rows(D_MODEL),
        out_shape=jax.ShapeDtypeStruct((n, D_MODEL), F32),
        compiler_params=_cparams(1),
        name="attention_context",
    )(qa, ka, va, qb, kb, vb, qc, kc, x, mod, wo, ln_g, ln_b, lp, sg, sink, wuv)


def _attn_lat_kernel(qa_ref, qb_ref, qc_ref,
                     ka_ref, va_ref, kb_ref, vb_ref, kc_ref,
                     xka_ref, xva_ref, xkb_ref, xvb_ref, xkc_ref,
                     x_ref, gt_ref, wo_ref, g_ref, b_ref, lp_ref, sg_ref, sink_ref, wuv_ref,
                     o_ref, qa_scr, qc_scr, sa_scr, sc_scr, *, lam_init, layer, n_lat):
    tq = qa_ref.shape[0]
    n_ctx = xka_ref.shape[0]
    n_chunks = n_lat // KEY_CHUNK
    lam = _diff_lambda(lp_ref, lam_init)

    def key_span(c):
        return slice(n_ctx + c * KEY_CHUNK, n_ctx + (c + 1) * KEY_CHUNK)

    def key_rows(c):
        return slice(c * KEY_CHUNK, (c + 1) * KEY_CHUNK)

    def score_pass(q_scr, s_scr, k_ref, xk_ref):
        s = _dot_nt(q_scr[...], xk_ref[...])
        s_scr[:, :n_ctx] = s
        m_run = _fold_lanes(s, jnp.maximum)
        for c in range(n_chunks):
            s = _dot_nt(q_scr[...], k_ref[key_rows(c), :])
            s_scr[:, key_span(c)] = s
            m_run = jnp.maximum(m_run, _fold_lanes(s, jnp.maximum))
        return jnp.max(m_run, axis=-1, keepdims=True)

    def value_pass(s_scr, m, v_ref, xv_ref):
        e = jnp.exp2(s_scr[:, :n_ctx] - m)
        l_run = _fold_lanes(e, jnp.add)
        acc = _dot(e.astype(BF16), xv_ref[...])
        for c in range(n_chunks):
            e = jnp.exp2(s_scr[:, key_span(c)] - m)
            l_run = l_run + _fold_lanes(e, jnp.add)
            acc = acc + _dot(e.astype(BF16), v_ref[key_rows(c), :])
        return acc, jnp.sum(l_run, axis=-1, keepdims=True)

    qa = qa_ref[...]
    for hj in range(2 * A_HEADS):
        qa_scr[hj * tq:(hj + 1) * tq, :] = qa * _lane_mask(256, hj * A_DQK, A_DQK, BF16)
    for h in range(C_HEADS):
        qc_scr[h * tq:(h + 1) * tq, :] = qc_ref[:, 256 * h:256 * (h + 1)]
    m_a = score_pass(qa_scr, sa_scr, ka_ref, xka_ref)
    m_c = score_pass(qc_scr, sc_scr, kc_ref, xkc_ref)

    acc, l = value_pass(sa_scr, m_a, va_ref, xva_ref)
    acc = acc / l
    o_a = jnp.zeros((tq, 256), F32)
    for h in range(A_HEADS):
        o_h = acc[2 * h * tq:(2 * h + 1) * tq] - lam * acc[(2 * h + 1) * tq:(2 * h + 2) * tq]
        o_a = _subln_accumulate(o_a, o_h, h, tq)
    o_a = o_a * sg_ref[...]

    i = pl.program_id(1)
    q0 = i * tq
    win = tq + 2 * WINDOW
    start = pl.multiple_of(jnp.clip(q0 - WINDOW, 0, n_lat - win), BLOCK)
    kpos = start + lax.broadcasted_iota(jnp.int32, (1, win), 1)
    qpos = q0 + (lax.broadcasted_iota(jnp.int32, (B_GROUP * tq, 1), 0) & (tq - 1))
    valid = jnp.abs(kpos - qpos) <= WINDOW
    qb = qb_ref[...]
    o_b = []
    for c in range(B_KV_HEADS):
        cs = slice(256 * c, 256 * c + 256)
        masks = [_lane_mask(256, g * B_DH, B_DH, BF16) for g in range(B_GROUP)]
        q4 = jnp.concatenate([qb[:, cs] * mk for mk in masks], axis=0)
        sink = jnp.concatenate([jnp.full((tq, 1), sink_ref[layer, c * B_GROUP + g] * LOG2E, F32)
                                for g in range(B_GROUP)], axis=0)
        s_ctx = _dot_nt(q4, xkb_ref[:, cs])
        s_loc = jnp.where(valid, _dot_nt(q4, kb_ref[pl.ds(start, win), cs]), NEG_BIG)
        m = jnp.maximum(jnp.maximum(jnp.max(s_ctx, axis=-1, keepdims=True),
                                    jnp.max(s_loc, axis=-1, keepdims=True)), sink)
        e_ctx = jnp.exp2(s_ctx - m)
        e_loc = jnp.exp2(s_loc - m)
        den = (jnp.sum(e_ctx, axis=-1, keepdims=True) + jnp.sum(e_loc, axis=-1, keepdims=True)
               + jnp.exp2(sink - m))
        acc = (_dot(e_ctx.astype(BF16), xvb_ref[:, cs])
               + _dot(e_loc.astype(BF16), vb_ref[pl.ds(start, win), cs])) / den
        oc = acc[:tq] * masks[0].astype(F32)
        for g in range(1, B_GROUP):
            oc = oc + acc[g * tq:(g + 1) * tq] * masks[g].astype(F32)
        o_b.append(oc)

    acc, l = value_pass(sc_scr, m_c, kc_ref, xkc_ref)
    acc = acc / l
    o_lat = jnp.concatenate([acc[h * tq:(h + 1) * tq, :C_KV_RANK] for h in range(C_HEADS)], axis=1)
    o_c = _dot(o_lat.astype(BF16), wuv_ref[...])

    _finish([o_a] + o_b + [o_c], x_ref, gt_ref, wo_ref, g_ref, b_ref, o_ref)


def _attn_lat(proj, ctx, x, mod, layer, wo, ln_g, ln_b, lp, sg, sink, wuv, lam_init, n_batch, n_lat):
    qa, ka, va, qb, kb, vb, qc, kc = proj
    xka, xva, xkb, xvb, xkc = ctx
    tq = BLOCK
    tiles = n_lat // tq
    n_ctx = xka.shape[2]

    def qrows(width):
        return pl.BlockSpec((tq, width), lambda bi, i: (bi * tiles + i, 0))

    def seq(width):
        return pl.BlockSpec((n_lat, width), lambda bi, i: (bi, 0), pipeline_mode=pl.Buffered(1))

    def cseq(width):
        return pl.BlockSpec((None, None, n_ctx, width), lambda bi, i: (bi, layer, 0, 0))

    return pl.pallas_call(
        functools.partial(_attn_lat_kernel, lam_init=lam_init, layer=layer, n_lat=n_lat),
        grid=(n_batch, tiles),
        in_specs=[qrows(256), qrows(512), qrows(1024),
                  seq(256), seq(256), seq(512), seq(512), seq(256),
                  cseq(256), cseq(256), cseq(512), cseq(512), cseq(256),
                  qrows(D_MODEL),
                  pl.BlockSpec((None, None, None, 1, D_MODEL), lambda bi, i: (layer, 1 + bi, 5, 0, 0)),
                  _pick(wo, layer), _pick(ln_g, layer, 1), _pick(ln_b, layer, 1), _pick(lp, layer),
                  _pick(sg, layer), pl.BlockSpec(memory_space=pltpu.SMEM), _pick(wuv, layer)],
        out_specs=qrows(D_MODEL),
        out_shape=jax.ShapeDtypeStruct((n_batch * n_lat, D_MODEL), F32),
        scratch_shapes=[pltpu.VMEM((2 * A_HEADS * tq, 256), BF16),
                        pltpu.VMEM((C_HEADS * tq, 256), BF16),
                        pltpu.VMEM((2 * A_HEADS * tq, n_ctx + n_lat), F32),
                        pltpu.VMEM((C_HEADS * tq, n_ctx + n_lat), F32)],
        compiler_params=_cparams(2),
        name="attention_latent",
    )(qa, qb, qc, ka, va, kb, vb, kc, xka, xva, xkb, xvb, xkc, x, mod, wo, ln_g, ln_b, lp, sg, sink, wuv)


def _rope_tables(rows, dim):
    row = jnp.repeat(jnp.arange(rows, dtype=F32), GRID_W)
    col = jnp.tile(jnp.arange(GRID_W, dtype=F32), rows)
    a = dim // 2
    inv = jnp.power(ROPE_BASE, -jnp.arange(0, a, 2, dtype=F32) / a)
    ar = row[:, None] * inv[None, :]
    ac = col[:, None] * inv[None, :]
    ang = jnp.concatenate([ar, ar, ac, ac], axis=-1)
    reps = LANES // dim
    cos = jnp.tile(jnp.cos(ang), (1, reps))
    sin = jnp.tile(jnp.sin(ang), (1, reps))
    lower = (np.arange(LANES) % (dim // 2)) < (dim // 4)
    sin_lo = jnp.where(lower[None, :], -sin, 0.0)
    sin_hi = jnp.where(lower[None, :], 0.0, sin)
    return cos, sin_lo, sin_hi


def _c_weights(w_q_up, w_kv_up):
    n_l = w_q_up.shape[0]
    wq = w_q_up.reshape(n_l, C_Q_RANK, C_HEADS, C_NOPE + C_ROPE)
    wqu = jnp.concatenate([wq[..., :C_NOPE].reshape(n_l, C_Q_RANK, -1),
                           wq[..., C_NOPE:].reshape(n_l, C_Q_RANK, -1)], axis=-1)
    eye_h = np.eye(C_HEADS, dtype=np.float32)
    w_uk = w_kv_up[..., :C_NOPE]
    w_uv = w_kv_up[..., C_NOPE:]
    top = jnp.einsum('lchd,hg->lhdgc', w_uk, eye_h)
    top = jnp.pad(top, ((0, 0),) * 4 + ((0, 256 - C_KV_RANK),)).reshape(n_l, C_HEADS * C_NOPE, C_HEADS * 256)
    place = np.zeros((C_HEADS, C_ROPE, C_HEADS, 256), np.float32)
    for h in range(C_HEADS):
        place[h, np.arange(C_ROPE), h, C_KV_RANK + np.arange(C_ROPE)] = 1.0
    bottom = jnp.broadcast_to(place.reshape(1, C_HEADS * C_ROPE, C_HEADS * 256), (n_l, C_HEADS * C_ROPE, C_HEADS * 256))
    wc2 = jnp.concatenate([top, bottom], axis=1)
    wuv = jnp.einsum('lchd,hg->lhcgd', w_uv, eye_h).reshape(n_l, C_HEADS * C_KV_RANK, C_HEADS * C_DV)
    return wqu.astype(BF16), wc2.astype(BF16), wuv.astype(BF16)


def _rep_kv(t):
    lead = t.shape[:3]
    return jnp.broadcast_to(t[..., None, :], lead + (B_KV_HEADS, B_GROUP, B_DH)).reshape(lead + (-1,))


def kernel(x_prompt, x_sample, cache_a_k, cache_a_v, cache_b_k, cache_b_v, cache_c_kv, cache_c_kpe, c, c_ctx,
           w_mod, b_mod, ln_g, ln_b, ffn_w1, ffn_w3, ffn_w2, w_in, w_o, a_lambda, a_subln_g, b_sink,
           c_q_norm_g, c_w_q_up, c_kv_norm_g, c_w_kv_up):
    n_req, seq, _ = x_prompt.shape
    n_dec, n_lat, _ = x_sample.shape
    n_ctx = cache_a_k.shape[2]
    assert n_lat % KEY_CHUNK == 0 and n_lat % TOKEN_TILE == 0 and (n_req * seq) % TOKEN_TILE == 0
    assert 1 + n_dec <= MOD_ROWS

    cond8 = jnp.concatenate([c_ctx[None, :], c, jnp.zeros((MOD_ROWS - 1 - n_dec, D_MODEL), F32)], axis=0)
    mod = _modulation(cond8, w_mod, b_mod[:, None, :]).reshape(DEPTH, MOD_ROWS, N_MOD, 1, D_MODEL)

    w1 = ffn_w1.astype(BF16)
    w3 = ffn_w3.astype(BF16)
    w2 = ffn_w2.astype(BF16)
    w_in_b = w_in.astype(BF16)
    w_kpe = jnp.pad(w_in[:, :, OFF_CKPE:], ((0, 0), (0, 0), (0, LANES - C_ROPE))).astype(BF16)
    wo = w_o.astype(BF16)
    g4 = ln_g.reshape(DEPTH, 3, 1, D_MODEL)
    b4 = ln_b.reshape(DEPTH, 3, 1, D_MODEL)
    qg = c_q_norm_g.reshape(DEPTH, 1, C_Q_RANK)
    kvg = c_kv_norm_g.reshape(DEPTH, 1, C_KV_RANK)
    wqu, wc2, wuv = _c_weights(c_w_q_up, c_w_kv_up)
    lam_inits = [0.8 - 0.6 * math.exp(-0.3 * l) for l in range(DEPTH)]
    sg = (jnp.tile(a_subln_g, (1, A_HEADS)) * (1.0 - np.asarray(lam_inits, np.float32))[:, None])[:, None, :]
    ctx_pad = jnp.zeros(cache_c_kv.shape[:3] + (256 - C_KV_RANK - C_ROPE,), F32)
    ctx = (cache_a_k.reshape(n_dec, DEPTH, n_ctx, -1), cache_a_v.reshape(n_dec, DEPTH, n_ctx, -1),
           _rep_kv(cache_b_k), _rep_kv(cache_b_v),
           jnp.concatenate([cache_c_kv, cache_c_kpe, ctx_pad], axis=-1))
    ctx = tuple(t.astype(BF16) for t in ctx)
    tabs = _rope_tables(n_lat // GRID_W, A_DQK) + _rope_tables(n_lat // GRID_W, B_DH)
    tiles_per_lat = n_lat // TOKEN_TILE

    def prompt_row(i):
        return 0

    def sample_row(i):
        return 1 + i // tiles_per_lat

    xp = x_prompt.reshape(n_req * seq, D_MODEL)
    xs = x_sample.reshape(n_dec * n_lat, D_MODEL)
    states = []
    for l in range(DEPTH):
        proj_w = (w_in_b, w_kpe, qg, wqu, wc2, kvg)
        attn_w = (wo, g4, b4, a_lambda, sg, b_sink, wuv, lam_inits[l])

        xp = _ffn(xp, mod, l, 0, w1, w3, w2, g4, b4, prompt_row)
        outs = _project(xp, mod, l, *proj_w, prompt_row, with_state=True)
        states.append(outs[8:])
        xp = _attn_ctx(outs[:8], xp, mod, l, *attn_w, seq)
        xp = _ffn(xp, mod, l, 1, w1, w3, w2, g4, b4, prompt_row)

        xs = _ffn(xs, mod, l, 0, w1, w3, w2, g4, b4, sample_row)
        outs = _project(xs, mod, l, *proj_w, sample_row, tabs=tabs)
        xs = _attn_lat(outs, ctx, xs, mod, l, *attn_w, n_dec, n_lat)
        xs = _ffn(xs, mod, l, 1, w1, w3, w2, g4, b4, sample_row)

    def stacked(k, tail):
        return jnp.stack([states[l][k].reshape((n_req, seq) + tail) for l in range(DEPTH)], axis=1)

    return (xp.reshape(n_req, seq, D_MODEL), xs.reshape(n_dec, n_lat, D_MODEL),
            stacked(0, (A_HEADS, 2 * A_DQK)), stacked(1, (A_HEADS, A_DV)),
            stacked(2, (B_KV_HEADS, B_DH)), stacked(3, (B_KV_HEADS, B_DH)),
            stacked(4, (C_KV_RANK,)), stacked(5, (C_ROPE,)))
```

```python
import functools
import math

import jax
import jax.numpy as jnp
import numpy as np
from jax import lax
from jax.experimental import pallas as pl
from jax.experimental.pallas import tpu as pltpu

D_MODEL = 1024
DEPTH = 2
GRID_W = 64
BLOCK = 128
A_HEADS = 4
A_DQK = 32
A_DV = 64
B_HEADS = 8
B_KV_HEADS = 2
B_GROUP = B_HEADS // B_KV_HEADS
B_DH = 64
WINDOW = 128
C_HEADS = 4
C_Q_RANK = 256
C_KV_RANK = 128
C_NOPE = 64
C_ROPE = 32
C_DV = 64
D_FF = 2816
N_MOD = 9
ROPE_BASE = 10000.0
LN_EPS = 1e-5
RMS_EPS = 1e-6
DN_ALPHA = (2 * DEPTH) ** 0.25
FFN_RES = 0.5
LOG2E = 1.4426950408889634

LANES = 128
MXU_DIM = 256
FF_CHUNK = MXU_DIM
N_FF_CHUNKS = D_FF // FF_CHUNK
TOKEN_TILE = 512
MOD_ROWS = 8
MOD_COL_TILE = 1024
KEY_CHUNK = 512
VMEM_LIMIT = 56 * 1024 * 1024
NEG_BIG = -1e30

OFF_AQ = 0
OFF_AK = 256
OFF_AV = 512
OFF_BQ = 768
OFF_BKV = 1280
OFF_CQD = 1536
OFF_CKV = 1792

BF16 = jnp.bfloat16
F32 = jnp.float32


def _dot(a, b):
    return jnp.dot(a, b, preferred_element_type=F32)


def _dot_nt(a, b):
    return lax.dot_general(a, b, (((1,), (1,)), ((), ())), preferred_element_type=F32)


def _cparams(n_grid):
    return pltpu.CompilerParams(dimension_semantics=("arbitrary",) * n_grid, vmem_limit_bytes=VMEM_LIMIT)


def _pick(arr, *lead):
    tail = arr.shape[len(lead):]
    zeros = (0,) * len(tail)
    return pl.BlockSpec((None,) * len(lead) + tail, lambda *_: lead + zeros, pipeline_mode=pl.Buffered(1))


def _mod_spec(layer, slot, row_fn):
    return pl.BlockSpec((None, None, None, 1, D_MODEL), lambda i, *_: (layer, row_fn(i), slot, 0, 0))


def _layernorm(y, g, b):
    mu = jnp.mean(y, axis=-1, keepdims=True)
    d = y - mu
    var = jnp.mean(d * d, axis=-1, keepdims=True)
    return d * lax.rsqrt(var + LN_EPS) * g + b


def _mod_kernel(c_ref, w_ref, b_ref, o_ref):
    c = c_ref[...]
    s = (c / (1.0 + jnp.exp(-c))).astype(BF16)
    o_ref[...] = _dot(s, w_ref[...].astype(BF16)) + b_ref[...]


def _modulation(cond8, w_mod, b_mod):
    n_col = N_MOD * D_MODEL
    return pl.pallas_call(
        _mod_kernel,
        grid=(DEPTH, n_col // MOD_COL_TILE),
        in_specs=[
            pl.BlockSpec((MOD_ROWS, D_MODEL), lambda l, j: (0, 0)),
            pl.BlockSpec((None, D_MODEL, MOD_COL_TILE), lambda l, j: (l, 0, j)),
            pl.BlockSpec((None, 1, MOD_COL_TILE), lambda l, j: (l, 0, j)),
        ],
        out_specs=pl.BlockSpec((None, MOD_ROWS, MOD_COL_TILE), lambda l, j: (l, 0, j)),
        out_shape=jax.ShapeDtypeStruct((DEPTH, MOD_ROWS, n_col), F32),
        compiler_params=_cparams(2),
        name="modulation",
    )(cond8, w_mod, b_mod)


def _ffn_kernel(x_ref, sh_ref, sc_ref, gt_ref, w1_ref, w3_ref, w2_ref, g_ref, b_ref, o_ref, h_scr):
    x = x_ref[...]
    h_scr[...] = (x * (1.0 + sc_ref[...]) + sh_ref[...]).astype(BF16)
    acc = None
    for c in range(N_FF_CHUNKS):
        cs = slice(c * FF_CHUNK, (c + 1) * FF_CHUNK)
        h = h_scr[...]
        a = _dot(h, w1_ref[:, cs])
        b = _dot(h, w3_ref[:, cs])
        u = (a / (1.0 + jnp.exp(-a))) * b
        d = _dot(u.astype(BF16), w2_ref[cs, :])
        acc = d if acc is None else acc + d
    y = DN_ALPHA * x + FFN_RES * gt_ref[...] * acc
    o_ref[...] = _layernorm(y, g_ref[...], b_ref[...])


def _ffn(x, mod, layer, half, w1, w3, w2, ln_g, ln_b, row_fn):
    n = x.shape[0]
    slot = 2 * half
    tile = pl.BlockSpec((TOKEN_TILE, D_MODEL), lambda i: (i, 0))
    return pl.pallas_call(
        _ffn_kernel,
        grid=(n // TOKEN_TILE,),
        in_specs=[
            tile,
            _mod_spec(layer, 3 * slot, row_fn), _mod_spec(layer, 3 * slot + 1, row_fn),
            _mod_spec(layer, 3 * slot + 2, row_fn),
            _pick(w1, layer, half), _pick(w3, layer, half), _pick(w2, layer, half),
            _pick(ln_g, layer, slot), _pick(ln_b, layer, slot),
        ],
        out_specs=tile,
        out_shape=jax.ShapeDtypeStruct((n, D_MODEL), F32),
        scratch_shapes=[pltpu.VMEM((TOKEN_TILE, D_MODEL), BF16)],
        compiler_params=_cparams(1),
        name="ffn_half",
    )(x, mod, mod, mod, w1, w3, w2, ln_g, ln_b)


def _rope(x, cos, sin_lo, sin_hi, q):
    return x * cos + pltpu.roll(x, LANES - q, 1) * sin_lo + pltpu.roll(x, q, 1) * sin_hi


def _rope_cols(x, tabs, q):
    cos, sin_lo, sin_hi = tabs
    parts = [_rope(x[:, j * LANES:(j + 1) * LANES], cos, sin_lo, sin_hi, q) for j in range(x.shape[1] // LANES)]
    return parts[0] if len(parts) == 1 else jnp.concatenate(parts, axis=1)


def _rms(x, g):
    return x * lax.rsqrt(jnp.mean(x * x, axis=-1, keepdims=True) + RMS_EPS) * g


def _repeat_kv_heads(t):
    lane = lax.broadcasted_iota(jnp.int32, (1, LANES), 1)
    first = lane < B_DH
    swapped = pltpu.roll(t, B_DH, 1)
    k0 = jnp.where(first, t, swapped)
    k1 = jnp.where(first, swapped, t)
    return jnp.concatenate([k0, k0, k1, k1], axis=1)


def _project_kernel(*refs, with_rope, with_state):
    (x_ref, sh_ref, sc_ref, win_ref, wck_ref, qg_ref, wqu_ref, wc2_ref, kvg_ref), rest = refs[:9], refs[9:]
    if with_rope:
        tab_refs, rest = rest[:6], rest[6:]
    (qa_ref, ka_ref, va_ref, qb_ref, kb_ref, vb_ref, qc_ref, kc_ref), rest = rest[:8], rest[8:]
    h_scr = rest[-1]
    rest = rest[:-1]

    h_scr[...] = (x_ref[...] * (1.0 + sc_ref[...]) + sh_ref[...]).astype(BF16)

    def piece(off, width):
        return _dot(h_scr[...], win_ref[:, off:off + width])

    a_q = piece(OFF_AQ, 256)
    a_k = piece(OFF_AK, 256)
    a_v = piece(OFF_AV, 256)
    b_q = piece(OFF_BQ, 512)
    b_kv = piece(OFF_BKV, 256)
    b_k = b_kv[:, :LANES]
    b_v = b_kv[:, LANES:]
    c_qd = piece(OFF_CQD, 256)
    c_k = _dot(h_scr[...], wck_ref[...])
    c_kvd = c_k[:, :LANES]
    c_kpe = c_k[:, LANES:]

    c_q = _dot(_rms(c_qd, qg_ref[...]).astype(BF16), wqu_ref[...])
    c_qn = c_q[:, :256]
    c_qp = c_q[:, 256:384]
    c_kv = _rms(c_kvd, kvg_ref[...])

    if with_state:
        sak_ref, sav_ref, sbk_ref, sbv_ref, sckv_ref, sckpe_ref = rest
        sak_ref[...] = a_k
        sav_ref[...] = a_v
        sbk_ref[...] = b_k
        sbv_ref[...] = b_v
        sckv_ref[...] = c_kv
        sckpe_ref[...] = c_kpe[:, :C_ROPE]

    if with_rope:
        ta = tuple(r[...] for r in tab_refs[:3])
        tb = tuple(r[...] for r in tab_refs[3:])
        lane = lax.broadcasted_iota(jnp.int32, (1, LANES), 1)
        keep = (lane < C_ROPE).astype(F32)
        tck = tuple(v * keep for v in ta)
        a_q = _rope_cols(a_q, ta, A_DQK // 4)
        a_k = _rope_cols(a_k, ta, A_DQK // 4)
        b_q = _rope_cols(b_q, tb, B_DH // 4)
        b_k = _rope_cols(b_k, tb, B_DH // 4)
        c_qp = _rope_cols(c_qp, ta, C_ROPE // 4)
        c_kpe = _rope_cols(c_kpe, tck, C_ROPE // 4)

    qa_ref[...] = (a_q * (A_DQK ** -0.5 * LOG2E)).astype(BF16)
    ka_ref[...] = a_k.astype(BF16)
    va_ref[...] = a_v.astype(BF16)
    qb_ref[...] = (b_q * (B_DH ** -0.5 * LOG2E)).astype(BF16)
    kb_ref[...] = _repeat_kv_heads(b_k).astype(BF16)
    vb_ref[...] = _repeat_kv_heads(b_v).astype(BF16)
    c_scale = (C_NOPE + C_ROPE) ** -0.5 * LOG2E
    c_qcat = jnp.concatenate([c_qn * c_scale, c_qp * c_scale], axis=1).astype(BF16)
    qc_ref[...] = _dot(c_qcat, wc2_ref[...]).astype(BF16)
    kc_ref[...] = jnp.concatenate([c_kv, c_kpe], axis=1).astype(BF16)


def _project(x, mod, layer, w_in, w_ck, qg, wqu, wc2, kvg, row_fn, tabs=None, with_state=False):
    n = x.shape[0]
    with_rope = tabs is not None

    def rows(width):
        return pl.BlockSpec((TOKEN_TILE, width), lambda i: (i, 0))

    in_specs = [rows(D_MODEL), _mod_spec(layer, 3, row_fn), _mod_spec(layer, 4, row_fn),
                _pick(w_in, layer), _pick(w_ck, layer), _pick(qg, layer), _pick(wqu, layer),
                _pick(wc2, layer), _pick(kvg, layer)]
    args = [x, mod, mod, w_in, w_ck, qg, wqu, wc2, kvg]
    if with_rope:
        tiles_per_seq = tabs[0].shape[0] // TOKEN_TILE
        in_specs += [pl.BlockSpec((TOKEN_TILE, LANES), lambda i: (i % tiles_per_seq, 0))] * 6
        args += list(tabs)
    widths = [256, 256, 256, 512, 512, 512, 1024, 256]
    out_specs = [rows(w) for w in widths]
    out_shape = [jax.ShapeDtypeStruct((n, w), BF16) for w in widths]
    if with_state:
        swidths = [256, 256, 128, 128, 128, C_ROPE]
        out_specs += [rows(w) for w in swidths]
        out_shape += [jax.ShapeDtypeStruct((n, w), F32) for w in swidths]
    return pl.pallas_call(
        functools.partial(_project_kernel, with_rope=with_rope, with_state=with_state),
        grid=(n // TOKEN_TILE,),
        in_specs=in_specs,
        out_specs=out_specs,
        out_shape=out_shape,
        scratch_shapes=[pltpu.VMEM((TOKEN_TILE, D_MODEL), BF16)],
        compiler_params=_cparams(1),
        name="project_latent" if with_rope else "project_context",
    )(*args)


def _lane_mask(width, lo, size, dtype):
    lane = lax.broadcasted_iota(jnp.int32, (1, width), 1)
    return ((lane >= lo) & (lane < lo + size)).astype(dtype)


def _fold_lanes(x, op):
    r = x[:, :LANES]
    for j in range(1, x.shape[1] // LANES):
        r = op(r, x[:, j * LANES:(j + 1) * LANES])
    return r


def _diff_lambda(lp_ref, lam_init):
    lp = lp_ref[...]
    s1 = jnp.sum(lp[0:1, :] * lp[1:2, :], axis=-1, keepdims=True)
    s2 = jnp.sum(lp[2:3, :] * lp[3:4, :], axis=-1, keepdims=True)
    return jnp.exp(s1) - jnp.exp(s2) + lam_init


def _subln_accumulate(o_a, o_full, head, tq):
    mask = _lane_mask(256, head * A_DV, A_DV, F32)
    om = o_full * mask
    ms = jnp.sum(om * om, axis=-1, keepdims=True) * (1.0 / A_DV)
    return o_a + om * lax.rsqrt(ms + RMS_EPS)


def _diff_heads(acc, lam, sg_ref, tq):
    o_a = jnp.zeros((tq, 256), F32)
    for h in range(A_HEADS):
        o_h = acc[2 * h * tq:(2 * h + 1) * tq] - lam * acc[(2 * h + 1) * tq:(2 * h + 2) * tq]
        o_a = _subln_accumulate(o_a, o_h, h, tq)
    return o_a * sg_ref[...]


def _group_heads(acc, masks, tq):
    oc = acc[:tq] * masks[0].astype(F32)
    for g in range(1, B_GROUP):
        oc = oc + acc[g * tq:(g + 1) * tq] * masks[g].astype(F32)
    return oc


def _finish(o_parts, x_ref, gt_ref, wo_ref, g_ref, b_ref, o_ref):
    o = jnp.concatenate(o_parts, axis=1).astype(BF16)
    f = _dot(o, wo_ref[...])
    y = DN_ALPHA * x_ref[...] + gt_ref[...] * f
    o_ref[...] = _layernorm(y, g_ref[...], b_ref[...])


def _attn_ctx_kernel(qa_ref, ka_ref, va_ref, qb_ref, kb_ref, vb_ref, qc_ref, kc_ref,
                     x_ref, gt_ref, wo_ref, g_ref, b_ref, lp_ref, sg_ref, sink_ref, wuvt_ref,
                     o_ref, *, lam_init, layer):
    tq = qa_ref.shape[0]
    lam = _diff_lambda(lp_ref, lam_init)

    def softmax_t(s_t, sink=None):
        m = jnp.max(s_t, axis=0, keepdims=True)
        if sink is not None:
            m = jnp.maximum(m, sink)
        e = jnp.exp2(s_t - m)
        l = jnp.sum(e, axis=0, keepdims=True)
        if sink is not None:
            l = l + jnp.exp2(sink - m)
        return e.astype(BF16), l

    def values_t(v, e, l):
        return _dot(v.T, e) / l

    qa = qa_ref[...]
    q8 = jnp.concatenate([qa * _lane_mask(256, hj * A_DQK, A_DQK, BF16) for hj in range(2 * A_HEADS)], axis=0)
    e, l = softmax_t(_dot_nt(ka_ref[...], q8))
    o_t = values_t(va_ref[...], e, l)
    heads = []
    for h in range(A_HEADS):
        rows = slice(A_DV * h, A_DV * (h + 1))
        o_h = o_t[rows, 2 * h * tq:(2 * h + 1) * tq] - lam * o_t[rows, (2 * h + 1) * tq:(2 * h + 2) * tq]
        ms = jnp.mean(o_h * o_h, axis=0, keepdims=True)
        heads.append(o_h * lax.rsqrt(ms + RMS_EPS))
    parts = [jnp.concatenate(heads, axis=0) * sg_ref[...]]

    qb = qb_ref[...]
    kb = kb_ref[...]
    vb = vb_ref[...]
    for c in range(B_KV_HEADS):
        cs = slice(256 * c, 256 * c + 256)
        q4 = jnp.concatenate([qb[:, cs] * _lane_mask(256, g * B_DH, B_DH, BF16) for g in range(B_GROUP)], axis=0)
        sink = jnp.concatenate([jnp.full((1, tq), sink_ref[layer, c * B_GROUP + g] * LOG2E, F32)
                                for g in range(B_GROUP)], axis=1)
        e, den = softmax_t(_dot_nt(kb[:, cs], q4), sink)
        o_t = values_t(vb[:, cs], e, den)
        parts.append(jnp.concatenate([o_t[B_DH * g:B_DH * (g + 1), g * tq:(g + 1) * tq] for g in range(B_GROUP)],
                                     axis=0))

    qc = qc_ref[...]
    kc = kc_ref[...]
    q4 = jnp.concatenate([qc[:, 256 * h:256 * h + 256] for h in range(C_HEADS)], axis=0)
    e, l = softmax_t(_dot_nt(kc, q4))
    o_t = values_t(kc, e, l)
    o_lat_t = jnp.concatenate([o_t[:C_KV_RANK, h * tq:(h + 1) * tq] for h in range(C_HEADS)], axis=0)
    parts.append(_dot(wuvt_ref[...], o_lat_t.astype(BF16)))

    o = jnp.concatenate(parts, axis=0).T.astype(BF16)
    f = _dot(o, wo_ref[...])
    y = DN_ALPHA * x_ref[...] + gt_ref[...] * f
    o_ref[...] = _layernorm(y, g_ref[...], b_ref[...])


def _attn_ctx(proj, x, mod, layer, wo, ln_g, ln_b, lp, sg, sink, wuv, lam_init, seq):
    qa, ka, va, qb, kb, vb, qc, kc = proj
    n = x.shape[0]

    def rows(width):
        return pl.BlockSpec((seq, width), lambda i: (i, 0))

    return pl.pallas_call(
        functools.partial(_attn_ctx_kernel, lam_init=lam_init, layer=layer),
        grid=(n // seq,),
        in_specs=[rows(256), rows(256), rows(256), rows(512), rows(512), rows(512), rows(1024), rows(256),
                  rows(D_MODEL), _mod_spec(layer, 5, lambda i: 0),
                  _pick(wo, layer), _pick(ln_g, layer, 1), _pick(ln_b, layer, 1), _pick(lp, layer),
                  _pick(sg, layer), pl.BlockSpec(memory_space=pltpu.SMEM), _pick(wuv, layer)],
        out_specs=rows(D_MODEL),
        out_shape=jax.ShapeDtypeStruct((n, D_MODEL), F32),
        compiler_params=_cparams(1),
        name="attention_context",
    )(qa, ka, va, qb, kb, vb, qc, kc, x, mod, wo, ln_g, ln_b, lp, sg, sink, wuv)


def _attn_lat_kernel(qa_ref, qb_ref, qc_ref,
                     ka_ref, va_ref, kb_ref, vb_ref, kc_ref,
                     xka_ref, xva_ref, xkb_ref, xvb_ref, xkc_ref,
                     x_ref, gt_ref, wo_ref, g_ref, b_ref, lp_ref, sg_ref, sink_ref, wuv_ref,
                     o_ref, qa_scr, qc_scr, sa_scr, sc_scr, *, lam_init, layer, n_lat):
    tq = qa_ref.shape[0]
    n_ctx = xka_ref.shape[0]
    n_chunks = n_lat // KEY_CHUNK
    lam = _diff_lambda(lp_ref, lam_init)

    def key_span(c):
        return slice(n_ctx + c * KEY_CHUNK, n_ctx + (c + 1) * KEY_CHUNK)

    def key_rows(c):
        return slice(c * KEY_CHUNK, (c + 1) * KEY_CHUNK)

    def score_pass(q_scr, s_scr, k_ref, xk_ref):
        s = _dot_nt(q_scr[...], xk_ref[...])
        s_scr[:, :n_ctx] = s
        m_run = _fold_lanes(s, jnp.maximum)
        for c in range(n_chunks):
            s = _dot_nt(q_scr[...], k_ref[key_rows(c), :])
            s_scr[:, key_span(c)] = s
            m_run = jnp.maximum(m_run, _fold_lanes(s, jnp.maximum))
        return jnp.max(m_run, axis=-1, keepdims=True)

    def value_pass(s_scr, m, v_ref, xv_ref):
        e = jnp.exp2(s_scr[:, :n_ctx] - m)
        l_run = _fold_lanes(e, jnp.add)
        acc = _dot(e.astype(BF16), xv_ref[...])
        for c in range(n_chunks):
            e = jnp.exp2(s_scr[:, key_span(c)] - m)
            l_run = l_run + _fold_lanes(e, jnp.add)
            acc = acc + _dot(e.astype(BF16), v_ref[key_rows(c), :])
        return acc / jnp.sum(l_run, axis=-1, keepdims=True)

    qa = qa_ref[...]
    for hj in range(2 * A_HEADS):
        qa_scr[hj * tq:(hj + 1) * tq, :] = qa * _lane_mask(256, hj * A_DQK, A_DQK, BF16)
    for h in range(C_HEADS):
        qc_scr[h * tq:(h + 1) * tq, :] = qc_ref[:, 256 * h:256 * (h + 1)]
    m_a = score_pass(qa_scr, sa_scr, ka_ref, xka_ref)
    m_c = score_pass(qc_scr, sc_scr, kc_ref, xkc_ref)

    o_a = _diff_heads(value_pass(sa_scr, m_a, va_ref, xva_ref), lam, sg_ref, tq)

    i = pl.program_id(1)
    q0 = i * tq
    win = tq + 2 * WINDOW
    start = pl.multiple_of(jnp.clip(q0 - WINDOW, 0, n_lat - win), BLOCK)
    kpos = start + lax.broadcasted_iota(jnp.int32, (1, win), 1)
    qpos = q0 + (lax.broadcasted_iota(jnp.int32, (B_GROUP * tq, 1), 0) & (tq - 1))
    valid = jnp.abs(kpos - qpos) <= WINDOW
    qb = qb_ref[...]
    o_b = []
    for c in range(B_KV_HEADS):
        cs = slice(256 * c, 256 * c + 256)
        masks = [_lane_mask(256, g * B_DH, B_DH, BF16) for g in range(B_GROUP)]
        q4 = jnp.concatenate([qb[:, cs] * mk for mk in masks], axis=0)
        sink = jnp.concatenate([jnp.full((tq, 1), sink_ref[layer, c * B_GROUP + g] * LOG2E, F32)
                                for g in range(B_GROUP)], axis=0)
        s_ctx = _dot_nt(q4, xkb_ref[:, cs])
        s_loc = jnp.where(valid, _dot_nt(q4, kb_ref[pl.ds(start, win), cs]), NEG_BIG)
        m_fold = jnp.maximum(_fold_lanes(s_ctx, jnp.maximum), _fold_lanes(s_loc, jnp.maximum))
        m = jnp.maximum(jnp.max(m_fold, axis=-1, keepdims=True), sink)
        e_ctx = jnp.exp2(s_ctx - m)
        e_loc = jnp.exp2(s_loc - m)
        l_fold = _fold_lanes(e_ctx, jnp.add) + _fold_lanes(e_loc, jnp.add)
        den = jnp.sum(l_fold, axis=-1, keepdims=True) + jnp.exp2(sink - m)
        acc = (_dot(e_ctx.astype(BF16), xvb_ref[:, cs])
               + _dot(e_loc.astype(BF16), vb_ref[pl.ds(start, win), cs])) / den
        o_b.append(_group_heads(acc, masks, tq))

    acc = value_pass(sc_scr, m_c, kc_ref, xkc_ref)
    o_lat = jnp.concatenate([acc[h * tq:(h + 1) * tq, :C_KV_RANK] for h in range(C_HEADS)], axis=1)
    o_c = _dot(o_lat.astype(BF16), wuv_ref[...])

    _finish([o_a] + o_b + [o_c], x_ref, gt_ref, wo_ref, g_ref, b_ref, o_ref)


def _attn_lat(proj, ctx, x, mod, layer, wo, ln_g, ln_b, lp, sg, sink, wuv, lam_init, n_batch, n_lat):
    qa, ka, va, qb, kb, vb, qc, kc = proj
    xka, xva, xkb, xvb, xkc = ctx
    tq = BLOCK
    tiles = n_lat // tq
    n_ctx = xka.shape[2]

    def qrows(width):
        return pl.BlockSpec((tq, width), lambda bi, i: (bi * tiles + i, 0))

    def seq(width):
        return pl.BlockSpec((n_lat, width), lambda bi, i: (bi, 0), pipeline_mode=pl.Buffered(1))

    def cseq(width):
        return pl.BlockSpec((None, None, n_ctx, width), lambda bi, i: (bi, layer, 0, 0))

    return pl.pallas_call(
        functools.partial(_attn_lat_kernel, lam_init=lam_init, layer=layer, n_lat=n_lat),
        grid=(n_batch, tiles),
        in_specs=[qrows(256), qrows(512), qrows(1024),
                  seq(256), seq(256), seq(512), seq(512), seq(256),
                  cseq(256), cseq(256), cseq(512), cseq(512), cseq(256),
                  qrows(D_MODEL),
                  pl.BlockSpec((None, None, None, 1, D_MODEL), lambda bi, i: (layer, 1 + bi, 5, 0, 0)),
                  _pick(wo, layer), _pick(ln_g, layer, 1), _pick(ln_b, layer, 1), _pick(lp, layer),
                  _pick(sg, layer), pl.BlockSpec(memory_space=pltpu.SMEM), _pick(wuv, layer)],
        out_specs=qrows(D_MODEL),
        out_shape=jax.ShapeDtypeStruct((n_batch * n_lat, D_MODEL), F32),
        scratch_shapes=[pltpu.VMEM((2 * A_HEADS * tq, 256), BF16),
                        pltpu.VMEM((C_HEADS * tq, 256), BF16),
                        pltpu.VMEM((2 * A_HEADS * tq, n_ctx + n_lat), F32),
                        pltpu.VMEM((C_HEADS * tq, n_ctx + n_lat), F32)],
        compiler_params=_cparams(2),
        name="attention_latent",
    )(qa, qb, qc, ka, va, kb, vb, kc, xka, xva, xkb, xvb, xkc, x, mod, wo, ln_g, ln_b, lp, sg, sink, wuv)


def _rope_tables(rows, dim):
    row = jnp.repeat(jnp.arange(rows, dtype=F32), GRID_W)
    col = jnp.tile(jnp.arange(GRID_W, dtype=F32), rows)
    a = dim // 2
    inv = jnp.power(ROPE_BASE, -jnp.arange(0, a, 2, dtype=F32) / a)
    ar = row[:, None] * inv[None, :]
    ac = col[:, None] * inv[None, :]
    ang = jnp.concatenate([ar, ar, ac, ac], axis=-1)
    reps = LANES // dim
    cos = jnp.tile(jnp.cos(ang), (1, reps))
    sin = jnp.tile(jnp.sin(ang), (1, reps))
    lower = (np.arange(LANES) % (dim // 2)) < (dim // 4)
    sin_lo = jnp.where(lower[None, :], -sin, 0.0)
    sin_hi = jnp.where(lower[None, :], 0.0, sin)
    return cos, sin_lo, sin_hi


def _c_weights(w_q_up, w_kv_up):
    n_l = w_q_up.shape[0]
    wq = w_q_up.reshape(n_l, C_Q_RANK, C_HEADS, C_NOPE + C_ROPE)
    wqu = jnp.concatenate([wq[..., :C_NOPE].reshape(n_l, C_Q_RANK, -1),
                           wq[..., C_NOPE:].reshape(n_l, C_Q_RANK, -1)], axis=-1)
    eye_h = np.eye(C_HEADS, dtype=np.float32)
    w_uk = w_kv_up[..., :C_NOPE]
    w_uv = w_kv_up[..., C_NOPE:]
    top = jnp.einsum('lchd,hg->lhdgc', w_uk, eye_h)
    top = jnp.pad(top, ((0, 0),) * 4 + ((0, 256 - C_KV_RANK),)).reshape(n_l, C_HEADS * C_NOPE, C_HEADS * 256)
    place = np.zeros((C_HEADS, C_ROPE, C_HEADS, 256), np.float32)
    for h in range(C_HEADS):
        place[h, np.arange(C_ROPE), h, C_KV_RANK + np.arange(C_ROPE)] = 1.0
    bottom = jnp.broadcast_to(place.reshape(1, C_HEADS * C_ROPE, C_HEADS * 256), (n_l, C_HEADS * C_ROPE, C_HEADS * 256))
    wc2 = jnp.concatenate([top, bottom], axis=1)
    wuv = jnp.einsum('lchd,hg->lhcgd', w_uv, eye_h).reshape(n_l, C_HEADS * C_KV_RANK, C_HEADS * C_DV)
    return wqu.astype(BF16), wc2.astype(BF16), wuv.astype(BF16)


def _rep_kv(t):
    lead = t.shape[:3]
    return jnp.broadcast_to(t[..., None, :], lead + (B_KV_HEADS, B_GROUP, B_DH)).reshape(lead + (-1,))


def kernel(x_prompt, x_sample, cache_a_k, cache_a_v, cache_b_k, cache_b_v, cache_c_kv, cache_c_kpe, c, c_ctx,
           w_mod, b_mod, ln_g, ln_b, ffn_w1, ffn_w3, ffn_w2, w_in, w_o, a_lambda, a_subln_g, b_sink,
           c_q_norm_g, c_w_q_up, c_kv_norm_g, c_w_kv_up):
    n_req, seq, _ = x_prompt.shape
    n_dec, n_lat, _ = x_sample.shape
    n_ctx = cache_a_k.shape[2]
    assert n_lat % KEY_CHUNK == 0 and n_lat % TOKEN_TILE == 0 and (n_req * seq) % TOKEN_TILE == 0
    assert 1 + n_dec <= MOD_ROWS

    cond8 = jnp.concatenate([c_ctx[None, :], c, jnp.zeros((MOD_ROWS - 1 - n_dec, D_MODEL), F32)], axis=0)
    mod = _modulation(cond8, w_mod, b_mod[:, None, :]).reshape(DEPTH, MOD_ROWS, N_MOD, 1, D_MODEL)

    w1 = ffn_w1.astype(BF16)
    w3 = ffn_w3.astype(BF16)
    w2 = ffn_w2.astype(BF16)
    w_in_b = w_in.astype(BF16)
    w_ck = jnp.pad(w_in[:, :, OFF_CKV:], ((0, 0), (0, 0), (0, 256 - C_KV_RANK - C_ROPE))).astype(BF16)
    wo = w_o.astype(BF16)
    g4 = ln_g.reshape(DEPTH, 3, 1, D_MODEL)
    b4 = ln_b.reshape(DEPTH, 3, 1, D_MODEL)
    qg = c_q_norm_g.reshape(DEPTH, 1, C_Q_RANK)
    kvg = c_kv_norm_g.reshape(DEPTH, 1, C_KV_RANK)
    wqu, wc2, wuv = _c_weights(c_w_q_up, c_w_kv_up)
    lam_inits = [0.8 - 0.6 * math.exp(-0.3 * l) for l in range(DEPTH)]
    sg = (jnp.tile(a_subln_g, (1, A_HEADS)) * (1.0 - np.asarray(lam_inits, np.float32))[:, None])[:, None, :]
    sg_t = jnp.broadcast_to(sg[:, 0, :, None], (DEPTH, A_HEADS * A_DV, seq))
    wuv_t = jnp.swapaxes(wuv, 1, 2)
    ctx_pad = jnp.zeros(cache_c_kv.shape[:3] + (256 - C_KV_RANK - C_ROPE,), F32)
    ctx = (cache_a_k.reshape(n_dec, DEPTH, n_ctx, -1), cache_a_v.reshape(n_dec, DEPTH, n_ctx, -1),
           _rep_kv(cache_b_k), _rep_kv(cache_b_v),
           jnp.concatenate([cache_c_kv, cache_c_kpe, ctx_pad], axis=-1))
    ctx = tuple(t.astype(BF16) for t in ctx)
    tabs = _rope_tables(n_lat // GRID_W, A_DQK) + _rope_tables(n_lat // GRID_W, B_DH)
    tiles_per_lat = n_lat // TOKEN_TILE

    def prompt_row(i):
        return 0

    def sample_row(i):
        return 1 + i // tiles_per_lat

    xp = x_prompt.reshape(n_req * seq, D_MODEL)
    xs = x_sample.reshape(n_dec * n_lat, D_MODEL)
    states = []
    for l in range(DEPTH):
        proj_w = (w_in_b, w_ck, qg, wqu, wc2, kvg)
        attn_w = (wo, g4, b4, a_lambda, sg, b_sink, wuv, lam_inits[l])
        attn_w_ctx = (wo, g4, b4, a_lambda, sg_t, b_sink, wuv_t, lam_inits[l])

        xp = _ffn(xp, mod, l, 0, w1, w3, w2, g4, b4, prompt_row)
        outs = _project(xp, mod, l, *proj_w, prompt_row, with_state=True)
        states.append(outs[8:])
        xp = _attn_ctx(outs[:8], xp, mod, l, *attn_w_ctx, seq)
        xp = _ffn(xp, mod, l, 1, w1, w3, w2, g4, b4, prompt_row)

        xs = _ffn(xs, mod, l, 0, w1, w3, w2, g4, b4, sample_row)
        outs = _project(xs, mod, l, *proj_w, sample_row, tabs=tabs)
        xs = _attn_lat(outs, ctx, xs, mod, l, *attn_w, n_dec, n_lat)
        xs = _ffn(xs, mod, l, 1, w1, w3, w2, g4, b4, sample_row)

    def stacked(k, tail):
        return jnp.stack([states[l][k].reshape((n_req, seq) + tail) for l in range(DEPTH)], axis=1)

    return (xp.reshape(n_req, seq, D_MODEL), xs.reshape(n_dec, n_lat, D_MODEL),
            stacked(0, (A_HEADS, 2 * A_DQK)), stacked(1, (A_HEADS, A_DV)),
            stacked(2, (B_KV_HEADS, B_DH)), stacked(3, (B_KV_HEADS, B_DH)),
            stacked(4, (C_KV_RANK,)), stacked(5, (C_ROPE,)))
```

```python
import functools
import math

import jax
import jax.numpy as jnp
import numpy as np
from jax import lax
from jax.experimental import pallas as pl
from jax.experimental.pallas import tpu as pltpu

D_MODEL = 1024
DEPTH = 2
GRID_W = 64
BLOCK = 128
A_HEADS = 4
A_DQK = 32
A_DV = 64
B_HEADS = 8
B_KV_HEADS = 2
B_GROUP = B_HEADS // B_KV_HEADS
B_DH = 64
WINDOW = 128
C_HEADS = 4
C_Q_RANK = 256
C_KV_RANK = 128
C_NOPE = 64
C_ROPE = 32
C_DV = 64
D_FF = 2816
N_MOD = 9
ROPE_BASE = 10000.0
LN_EPS = 1e-5
RMS_EPS = 1e-6
DN_ALPHA = (2 * DEPTH) ** 0.25
FFN_RES = 0.5
LOG2E = 1.4426950408889634

LANES = 128
MXU_DIM = 256
FF_CHUNK = MXU_DIM
N_FF_CHUNKS = D_FF // FF_CHUNK
TOKEN_TILE = 512
MOD_ROWS = 8
MOD_COL_TILE = 1024
VMEM_LIMIT = 56 * 1024 * 1024
NEG_BIG = -1e30

OFF_AQ = 0
OFF_AK = 256
OFF_AV = 512
OFF_BQ = 768
OFF_BKV = 1280
OFF_CQD = 1536
OFF_CKV = 1792

BF16 = jnp.bfloat16
F32 = jnp.float32


def _dot(a, b):
    return jnp.dot(a, b, preferred_element_type=F32)


def _dot_nt(a, b):
    return lax.dot_general(a, b, (((1,), (1,)), ((), ())), preferred_element_type=F32)


def _cparams(n_grid):
    return pltpu.CompilerParams(dimension_semantics=("arbitrary",) * n_grid, vmem_limit_bytes=VMEM_LIMIT)


def _pick(arr, *lead):
    tail = arr.shape[len(lead):]
    zeros = (0,) * len(tail)
    return pl.BlockSpec((None,) * len(lead) + tail, lambda *_: lead + zeros, pipeline_mode=pl.Buffered(1))


def _mod_spec(layer, slot, row_fn):
    return pl.BlockSpec((None, None, None, 1, D_MODEL), lambda i, *_: (layer, row_fn(i), slot, 0, 0))


def _layernorm(y, g, b):
    mu = jnp.mean(y, axis=-1, keepdims=True)
    d = y - mu
    var = jnp.mean(d * d, axis=-1, keepdims=True)
    return d * lax.rsqrt(var + LN_EPS) * g + b


def _mod_kernel(c_ref, w_ref, b_ref, o_ref):
    c = c_ref[...]
    s = (c / (1.0 + jnp.exp(-c))).astype(BF16)
    o_ref[...] = _dot(s, w_ref[...].astype(BF16)) + b_ref[...]


def _modulation(cond8, w_mod, b_mod):
    n_col = N_MOD * D_MODEL
    return pl.pallas_call(
        _mod_kernel,
        grid=(DEPTH, n_col // MOD_COL_TILE),
        in_specs=[
            pl.BlockSpec((MOD_ROWS, D_MODEL), lambda l, j: (0, 0)),
            pl.BlockSpec((None, D_MODEL, MOD_COL_TILE), lambda l, j: (l, 0, j)),
            pl.BlockSpec((None, 1, MOD_COL_TILE), lambda l, j: (l, 0, j)),
        ],
        out_specs=pl.BlockSpec((None, MOD_ROWS, MOD_COL_TILE), lambda l, j: (l, 0, j)),
        out_shape=jax.ShapeDtypeStruct((DEPTH, MOD_ROWS, n_col), F32),
        compiler_params=_cparams(2),
        name="modulation",
    )(cond8, w_mod, b_mod)


def _ffn_kernel(x_ref, sh_ref, sc_ref, gt_ref, w1_ref, w3_ref, w2_ref, g_ref, b_ref, o_ref, h_scr):
    x = x_ref[...]
    h_scr[...] = (x * (1.0 + sc_ref[...]) + sh_ref[...]).astype(BF16)
    acc = None
    for c in range(N_FF_CHUNKS):
        cs = slice(c * FF_CHUNK, (c + 1) * FF_CHUNK)
        h = h_scr[...]
        a = _dot(h, w1_ref[:, cs])
        b = _dot(h, w3_ref[:, cs])
        u = (a / (1.0 + jnp.exp(-a))) * b
        d = _dot(u.astype(BF16), w2_ref[cs, :])
        acc = d if acc is None else acc + d
    y = DN_ALPHA * x + FFN_RES * gt_ref[...] * acc
    o_ref[...] = _layernorm(y, g_ref[...], b_ref[...])


def _ffn(x, mod, layer, half, w1, w3, w2, ln_g, ln_b, row_fn):
    n = x.shape[0]
    slot = 2 * half
    tile = pl.BlockSpec((TOKEN_TILE, D_MODEL), lambda i: (i, 0))
    return pl.pallas_call(
        _ffn_kernel,
        grid=(n // TOKEN_TILE,),
        in_specs=[
            tile,
            _mod_spec(layer, 3 * slot, row_fn), _mod_spec(layer, 3 * slot + 1, row_fn),
            _mod_spec(layer, 3 * slot + 2, row_fn),
            _pick(w1, layer, half), _pick(w3, layer, half), _pick(w2, layer, half),
            _pick(ln_g, layer, slot), _pick(ln_b, layer, slot),
        ],
        out_specs=tile,
        out_shape=jax.ShapeDtypeStruct((n, D_MODEL), F32),
        scratch_shapes=[pltpu.VMEM((TOKEN_TILE, D_MODEL), BF16)],
        compiler_params=_cparams(1),
        name="ffn_half",
    )(x, mod, mod, mod, w1, w3, w2, ln_g, ln_b)


def _rope(x, cos, sin_lo, sin_hi, q):
    return x * cos + pltpu.roll(x, LANES - q, 1) * sin_lo + pltpu.roll(x, q, 1) * sin_hi


def _rope_cols(x, tabs, q):
    cos, sin_lo, sin_hi = tabs
    parts = [_rope(x[:, j * LANES:(j + 1) * LANES], cos, sin_lo, sin_hi, q) for j in range(x.shape[1] // LANES)]
    return parts[0] if len(parts) == 1 else jnp.concatenate(parts, axis=1)


def _rms(x, g):
    return x * lax.rsqrt(jnp.mean(x * x, axis=-1, keepdims=True) + RMS_EPS) * g


def _repeat_kv_heads(t):
    lane = lax.broadcasted_iota(jnp.int32, (1, LANES), 1)
    first = lane < B_DH
    swapped = pltpu.roll(t, B_DH, 1)
    k0 = jnp.where(first, t, swapped)
    k1 = jnp.where(first, swapped, t)
    return jnp.concatenate([k0, k0, k1, k1], axis=1)


def _project_kernel(*refs, with_rope, with_state):
    (x_ref, sh_ref, sc_ref, win_ref, wck_ref, qg_ref, wqu_ref, wc2_ref, kvg_ref), rest = refs[:9], refs[9:]
    if with_rope:
        tab_refs, rest = rest[:6], rest[6:]
    (qa_ref, ka_ref, va_ref, qb_ref, kb_ref, vb_ref, qc_ref, kc_ref), rest = rest[:8], rest[8:]
    h_scr = rest[-1]
    rest = rest[:-1]
    if with_rope:
        ckvt_ref, rest = rest[0], rest[1:]

    h_scr[...] = (x_ref[...] * (1.0 + sc_ref[...]) + sh_ref[...]).astype(BF16)

    def piece(off, width):
        return _dot(h_scr[...], win_ref[:, off:off + width])

    a_q = piece(OFF_AQ, 256)
    a_k = piece(OFF_AK, 256)
    a_v = piece(OFF_AV, 256)
    b_q = piece(OFF_BQ, 512)
    b_kv = piece(OFF_BKV, 256)
    b_k = b_kv[:, :LANES]
    b_v = b_kv[:, LANES:]
    c_qd = piece(OFF_CQD, 256)
    c_k = _dot(h_scr[...], wck_ref[...])
    c_kvd = c_k[:, :LANES]
    c_kpe = c_k[:, LANES:]

    c_q = _dot(_rms(c_qd, qg_ref[...]).astype(BF16), wqu_ref[...])
    c_qn = c_q[:, :256]
    c_qp = c_q[:, 256:384]
    c_kv = _rms(c_kvd, kvg_ref[...])

    if with_state:
        sak_ref, sav_ref, sbk_ref, sbv_ref, sckv_ref, sckpe_ref = rest
        sak_ref[...] = a_k
        sav_ref[...] = a_v
        sbk_ref[...] = b_k
        sbv_ref[...] = b_v
        sckv_ref[...] = c_kv
        sckpe_ref[...] = c_kpe[:, :C_ROPE]

    if with_rope:
        ta = tuple(r[...] for r in tab_refs[:3])
        tb = tuple(r[...] for r in tab_refs[3:])
        lane = lax.broadcasted_iota(jnp.int32, (1, LANES), 1)
        keep = (lane < C_ROPE).astype(F32)
        tck = tuple(v * keep for v in ta)
        a_q = _rope_cols(a_q, ta, A_DQK // 4)
        a_k = _rope_cols(a_k, ta, A_DQK // 4)
        b_q = _rope_cols(b_q, tb, B_DH // 4)
        b_k = _rope_cols(b_k, tb, B_DH // 4)
        c_qp = _rope_cols(c_qp, ta, C_ROPE // 4)
        c_kpe = _rope_cols(c_kpe, tck, C_ROPE // 4)

    qa_ref[...] = (a_q * (A_DQK ** -0.5 * LOG2E)).astype(BF16)
    ka_ref[...] = a_k.astype(BF16)
    if with_rope:
        va_ref[...] = a_v.astype(BF16).T
        ckvt_ref[...] = c_kv.astype(BF16).T
    else:
        va_ref[...] = a_v.astype(BF16)
    qb_ref[...] = (b_q * (B_DH ** -0.5 * LOG2E)).astype(BF16)
    kb_ref[...] = _repeat_kv_heads(b_k).astype(BF16)
    vb_ref[...] = _repeat_kv_heads(b_v).astype(BF16)
    c_scale = (C_NOPE + C_ROPE) ** -0.5 * LOG2E
    c_qcat = jnp.concatenate([c_qn * c_scale, c_qp * c_scale], axis=1).astype(BF16)
    qc_ref[...] = _dot(c_qcat, wc2_ref[...]).astype(BF16)
    kc_ref[...] = jnp.concatenate([c_kv, c_kpe], axis=1).astype(BF16)


def _project(x, mod, layer, w_in, w_ck, qg, wqu, wc2, kvg, row_fn, tabs=None, with_state=False):
    n = x.shape[0]
    with_rope = tabs is not None

    def rows(width):
        return pl.BlockSpec((TOKEN_TILE, width), lambda i: (i, 0))

    in_specs = [rows(D_MODEL), _mod_spec(layer, 3, row_fn), _mod_spec(layer, 4, row_fn),
                _pick(w_in, layer), _pick(w_ck, layer), _pick(qg, layer), _pick(wqu, layer),
                _pick(wc2, layer), _pick(kvg, layer)]
    args = [x, mod, mod, w_in, w_ck, qg, wqu, wc2, kvg]
    if with_rope:
        tiles_per_seq = tabs[0].shape[0] // TOKEN_TILE
        in_specs += [pl.BlockSpec((TOKEN_TILE, LANES), lambda i: (i % tiles_per_seq, 0))] * 6
        args += list(tabs)
    widths = [256, 256, 256, 512, 512, 512, 1024, 256]
    out_specs = [rows(w) for w in widths]
    out_shape = [jax.ShapeDtypeStruct((n, w), BF16) for w in widths]
    if with_rope:
        def cols(width):
            return pl.BlockSpec((width, TOKEN_TILE), lambda i: (0, i))

        out_specs[2] = cols(256)
        out_shape[2] = jax.ShapeDtypeStruct((256, n), BF16)
        out_specs.append(cols(C_KV_RANK))
        out_shape.append(jax.ShapeDtypeStruct((C_KV_RANK, n), BF16))
    if with_state:
        swidths = [256, 256, 128, 128, 128, C_ROPE]
        out_specs += [rows(w) for w in swidths]
        out_shape += [jax.ShapeDtypeStruct((n, w), F32) for w in swidths]
    return pl.pallas_call(
        functools.partial(_project_kernel, with_rope=with_rope, with_state=with_state),
        grid=(n // TOKEN_TILE,),
        in_specs=in_specs,
        out_specs=out_specs,
        out_shape=out_shape,
        scratch_shapes=[pltpu.VMEM((TOKEN_TILE, D_MODEL), BF16)],
        compiler_params=_cparams(1),
        name="project_latent" if with_rope else "project_context",
    )(*args)


def _lane_mask(width, lo, size, dtype):
    lane = lax.broadcasted_iota(jnp.int32, (1, width), 1)
    return ((lane >= lo) & (lane < lo + size)).astype(dtype)


def _diff_lambda(lp_ref, lam_init):
    lp = lp_ref[...]
    s1 = jnp.sum(lp[0:1, :] * lp[1:2, :], axis=-1, keepdims=True)
    s2 = jnp.sum(lp[2:3, :] * lp[3:4, :], axis=-1, keepdims=True)
    return jnp.exp(s1) - jnp.exp(s2) + lam_init


def _diff_head_t(blk, lam, tq):
    o_h = blk[:, :tq] - lam * blk[:, tq:]
    ms = jnp.mean(o_h * o_h, axis=0, keepdims=True)
    return o_h * lax.rsqrt(ms + RMS_EPS)


def _group_heads_t(o_t, tq):
    return jnp.concatenate([o_t[B_DH * g:B_DH * (g + 1), g * tq:(g + 1) * tq] for g in range(B_GROUP)], axis=0)


def _sink_row(sink_ref, layer, kv_head, tq):
    return jnp.concatenate([jnp.full((1, tq), sink_ref[layer, kv_head * B_GROUP + g] * LOG2E, F32)
                            for g in range(B_GROUP)], axis=1)


def _finish_t(parts, x_ref, gt_ref, wo_ref, g_ref, b_ref, o_ref):
    o = jnp.concatenate(parts, axis=0).T.astype(BF16)
    f = _dot(o, wo_ref[...])
    y = DN_ALPHA * x_ref[...] + gt_ref[...] * f
    o_ref[...] = _layernorm(y, g_ref[...], b_ref[...])


def _attn_ctx_kernel(qa_ref, ka_ref, va_ref, qb_ref, kb_ref, vb_ref, qc_ref, kc_ref,
                     x_ref, gt_ref, wo_ref, g_ref, b_ref, lp_ref, sg_ref, sink_ref, wuvt_ref,
                     o_ref, *, lam_init, layer):
    tq = qa_ref.shape[0]
    lam = _diff_lambda(lp_ref, lam_init)

    def softmax_t(s_t, sink=None):
        m = jnp.max(s_t, axis=0, keepdims=True)
        if sink is not None:
            m = jnp.maximum(m, sink)
        e = jnp.exp2(s_t - m)
        l = jnp.sum(e, axis=0, keepdims=True)
        if sink is not None:
            l = l + jnp.exp2(sink - m)
        return e.astype(BF16), l

    def values_t(v, e, l):
        return _dot(v.T, e) / l

    qa = qa_ref[...]
    q8 = jnp.concatenate([qa * _lane_mask(256, hj * A_DQK, A_DQK, BF16) for hj in range(2 * A_HEADS)], axis=0)
    e, l = softmax_t(_dot_nt(ka_ref[...], q8))
    o_t = values_t(va_ref[...], e, l)
    heads = [_diff_head_t(o_t[A_DV * h:A_DV * (h + 1), 2 * h * tq:(2 * h + 2) * tq], lam, tq)
             for h in range(A_HEADS)]
    parts = [jnp.concatenate(heads, axis=0) * sg_ref[...]]

    qb = qb_ref[...]
    kb = kb_ref[...]
    vb = vb_ref[...]
    for c in range(B_KV_HEADS):
        cs = slice(256 * c, 256 * c + 256)
        q4 = jnp.concatenate([qb[:, cs] * _lane_mask(256, g * B_DH, B_DH, BF16) for g in range(B_GROUP)], axis=0)
        e, den = softmax_t(_dot_nt(kb[:, cs], q4), _sink_row(sink_ref, layer, c, tq))
        parts.append(_group_heads_t(values_t(vb[:, cs], e, den), tq))

    qc = qc_ref[...]
    kc = kc_ref[...]
    q4 = jnp.concatenate([qc[:, 256 * h:256 * h + 256] for h in range(C_HEADS)], axis=0)
    e, l = softmax_t(_dot_nt(kc, q4))
    o_t = values_t(kc, e, l)
    o_lat_t = jnp.concatenate([o_t[:C_KV_RANK, h * tq:(h + 1) * tq] for h in range(C_HEADS)], axis=0)
    parts.append(_dot(wuvt_ref[...], o_lat_t.astype(BF16)))
    _finish_t(parts, x_ref, gt_ref, wo_ref, g_ref, b_ref, o_ref)


def _attn_ctx(proj, x, mod, layer, wo, ln_g, ln_b, lp, sg, sink, wuv, lam_init, seq):
    qa, ka, va, qb, kb, vb, qc, kc = proj
    n = x.shape[0]

    def rows(width):
        return pl.BlockSpec((seq, width), lambda i: (i, 0))

    return pl.pallas_call(
        functools.partial(_attn_ctx_kernel, lam_init=lam_init, layer=layer),
        grid=(n // seq,),
        in_specs=[rows(256), rows(256), rows(256), rows(512), rows(512), rows(512), rows(1024), rows(256),
                  rows(D_MODEL), _mod_spec(layer, 5, lambda i: 0),
                  _pick(wo, layer), _pick(ln_g, layer, 1), _pick(ln_b, layer, 1), _pick(lp, layer),
                  _pick(sg, layer), pl.BlockSpec(memory_space=pltpu.SMEM), _pick(wuv, layer)],
        out_specs=rows(D_MODEL),
        out_shape=jax.ShapeDtypeStruct((n, D_MODEL), F32),
        compiler_params=_cparams(1),
        name="attention_context",
    )(qa, ka, va, qb, kb, vb, qc, kc, x, mod, wo, ln_g, ln_b, lp, sg, sink, wuv)


def _attn_lat_kernel(qa_ref, qb_ref, qc_ref,
                     ka_ref, vat_ref, kb_ref, vb_ref, kc_ref, ckvt_ref,
                     xka_ref, xvat_ref, xkb_ref, xvb_ref, xkc_ref, xckvt_ref,
                     x_ref, gt_ref, wo_ref, g_ref, b_ref, lp_ref, sgt_ref, sink_ref, wuvt_ref,
                     o_ref, *, lam_init, layer, n_lat):
    tq = qa_ref.shape[0]
    lam = _diff_lambda(lp_ref, lam_init)

    def attend_t(q2, k_ref, xk_ref, vt, xvt):
        s_x = _dot_nt(xk_ref[...], q2)
        s_l = _dot_nt(k_ref[...], q2)
        m = jnp.maximum(jnp.max(s_x, axis=0, keepdims=True), jnp.max(s_l, axis=0, keepdims=True))
        e_x = jnp.exp2(s_x - m)
        e_l = jnp.exp2(s_l - m)
        l = jnp.sum(e_x, axis=0, keepdims=True) + jnp.sum(e_l, axis=0, keepdims=True)
        return (_dot(xvt, e_x.astype(BF16)) + _dot(vt, e_l.astype(BF16))) / l

    qa = qa_ref[...]
    heads = []
    for hp in range(A_HEADS // 2):
        q4 = jnp.concatenate([qa * _lane_mask(256, (4 * hp + k) * A_DQK, A_DQK, BF16) for k in range(4)], axis=0)
        rows = slice(2 * A_DV * hp, 2 * A_DV * (hp + 1))
        o_t = attend_t(q4, ka_ref, xka_ref, vat_ref[rows, :], xvat_ref[rows, :])
        heads += [_diff_head_t(o_t[A_DV * k:A_DV * (k + 1), 2 * k * tq:(2 * k + 2) * tq], lam, tq) for k in range(2)]
    parts = [jnp.concatenate(heads, axis=0) * sgt_ref[...]]

    i = pl.program_id(1)
    q0 = i * tq
    win = tq + 2 * WINDOW
    start = pl.multiple_of(jnp.clip(q0 - WINDOW, 0, n_lat - win), BLOCK)
    kpos = start + lax.broadcasted_iota(jnp.int32, (win, 1), 0)
    qpos = q0 + (lax.broadcasted_iota(jnp.int32, (1, B_GROUP * tq), 1) & (tq - 1))
    valid = jnp.abs(kpos - qpos) <= WINDOW
    qb = qb_ref[...]
    for c in range(B_KV_HEADS):
        cs = slice(256 * c, 256 * c + 256)
        q4 = jnp.concatenate([qb[:, cs] * _lane_mask(256, g * B_DH, B_DH, BF16) for g in range(B_GROUP)], axis=0)
        sink = _sink_row(sink_ref, layer, c, tq)
        s_x = _dot_nt(xkb_ref[:, cs], q4)
        s_l = jnp.where(valid, _dot_nt(kb_ref[pl.ds(start, win), cs], q4), NEG_BIG)
        m = jnp.maximum(jnp.maximum(jnp.max(s_x, axis=0, keepdims=True), jnp.max(s_l, axis=0, keepdims=True)), sink)
        e_x = jnp.exp2(s_x - m)
        e_l = jnp.exp2(s_l - m)
        den = jnp.sum(e_x, axis=0, keepdims=True) + jnp.sum(e_l, axis=0, keepdims=True) + jnp.exp2(sink - m)
        o_t = (_dot(xvb_ref[:, cs].T, e_x.astype(BF16))
               + _dot(vb_ref[pl.ds(start, win), cs].T, e_l.astype(BF16))) / den
        parts.append(_group_heads_t(o_t, tq))

    q4 = jnp.concatenate([qc_ref[:, 256 * h:256 * (h + 1)] for h in range(C_HEADS)], axis=0)
    o_t = attend_t(q4, kc_ref, xkc_ref, ckvt_ref[...], xckvt_ref[...])
    o_lat_t = jnp.concatenate([o_t[:, h * tq:(h + 1) * tq] for h in range(C_HEADS)], axis=0)
    parts.append(_dot(wuvt_ref[...], o_lat_t.astype(BF16)))
    _finish_t(parts, x_ref, gt_ref, wo_ref, g_ref, b_ref, o_ref)


def _attn_lat(proj, ctx, x, mod, layer, wo, ln_g, ln_b, lp, sg_t, sink, wuv_t, lam_init, n_batch, n_lat):
    qa, ka, va_t, qb, kb, vb, qc, kc, ckv_t = proj
    xka, xva_t, xkb, xvb, xkc, xckv_t = ctx
    tq = BLOCK
    tiles = n_lat // tq
    n_ctx = xka.shape[2]

    def qrows(width):
        return pl.BlockSpec((tq, width), lambda bi, i: (bi * tiles + i, 0))

    def seq(width):
        return pl.BlockSpec((n_lat, width), lambda bi, i: (bi, 0), pipeline_mode=pl.Buffered(1))

    def seq_t(width):
        return pl.BlockSpec((width, n_lat), lambda bi, i: (0, bi), pipeline_mode=pl.Buffered(1))

    def cseq(rows, width):
        return pl.BlockSpec((None, None, rows, width), lambda bi, i: (bi, layer, 0, 0))

    return pl.pallas_call(
        functools.partial(_attn_lat_kernel, lam_init=lam_init, layer=layer, n_lat=n_lat),
        grid=(n_batch, tiles),
        in_specs=[qrows(256), qrows(512), qrows(1024),
                  seq(256), seq_t(256), seq(512), seq(512), seq(256), seq_t(C_KV_RANK),
                  cseq(n_ctx, 256), cseq(256, n_ctx), cseq(n_ctx, 512), cseq(n_ctx, 512),
                  cseq(n_ctx, 256), cseq(C_KV_RANK, n_ctx),
                  qrows(D_MODEL),
                  pl.BlockSpec((None, None, None, 1, D_MODEL), lambda bi, i: (layer, 1 + bi, 5, 0, 0)),
                  _pick(wo, layer), _pick(ln_g, layer, 1), _pick(ln_b, layer, 1), _pick(lp, layer),
                  _pick(sg_t, layer), pl.BlockSpec(memory_space=pltpu.SMEM), _pick(wuv_t, layer)],
        out_specs=qrows(D_MODEL),
        out_shape=jax.ShapeDtypeStruct((n_batch * n_lat, D_MODEL), F32),
        compiler_params=_cparams(2),
        name="attention_latent",
    )(qa, qb, qc, ka, va_t, kb, vb, kc, ckv_t, xka, xva_t, xkb, xvb, xkc, xckv_t,
      x, mod, wo, ln_g, ln_b, lp, sg_t, sink, wuv_t)


def _rope_tables(rows, dim):
    row = jnp.repeat(jnp.arange(rows, dtype=F32), GRID_W)
    col = jnp.tile(jnp.arange(GRID_W, dtype=F32), rows)
    a = dim // 2
    inv = jnp.power(ROPE_BASE, -jnp.arange(0, a, 2, dtype=F32) / a)
    ar = row[:, None] * inv[None, :]
    ac = col[:, None] * inv[None, :]
    ang = jnp.concatenate([ar, ar, ac, ac], axis=-1)
    reps = LANES // dim
    cos = jnp.tile(jnp.cos(ang), (1, reps))
    sin = jnp.tile(jnp.sin(ang), (1, reps))
    lower = (np.arange(LANES) % (dim // 2)) < (dim // 4)
    sin_lo = jnp.where(lower[None, :], -sin, 0.0)
    sin_hi = jnp.where(lower[None, :], 0.0, sin)
    return cos, sin_lo, sin_hi


def _c_weights(w_q_up, w_kv_up):
    n_l = w_q_up.shape[0]
    wq = w_q_up.reshape(n_l, C_Q_RANK, C_HEADS, C_NOPE + C_ROPE)
    wqu = jnp.concatenate([wq[..., :C_NOPE].reshape(n_l, C_Q_RANK, -1),
                           wq[..., C_NOPE:].reshape(n_l, C_Q_RANK, -1)], axis=-1)
    eye_h = np.eye(C_HEADS, dtype=np.float32)
    w_uk = w_kv_up[..., :C_NOPE]
    w_uv = w_kv_up[..., C_NOPE:]
    top = jnp.einsum('lchd,hg->lhdgc', w_uk, eye_h)
    top = jnp.pad(top, ((0, 0),) * 4 + ((0, 256 - C_KV_RANK),)).reshape(n_l, C_HEADS * C_NOPE, C_HEADS * 256)
    place = np.zeros((C_HEADS, C_ROPE, C_HEADS, 256), np.float32)
    for h in range(C_HEADS):
        place[h, np.arange(C_ROPE), h, C_KV_RANK + np.arange(C_ROPE)] = 1.0
    bottom = jnp.broadcast_to(place.reshape(1, C_HEADS * C_ROPE, C_HEADS * 256), (n_l, C_HEADS * C_ROPE, C_HEADS * 256))
    wc2 = jnp.concatenate([top, bottom], axis=1)
    wuv = jnp.einsum('lchd,hg->lhcgd', w_uv, eye_h).reshape(n_l, C_HEADS * C_KV_RANK, C_HEADS * C_DV)
    return wqu.astype(BF16), wc2.astype(BF16), wuv.astype(BF16)


def _rep_kv(t):
    lead = t.shape[:3]
    return jnp.broadcast_to(t[..., None, :], lead + (B_KV_HEADS, B_GROUP, B_DH)).reshape(lead + (-1,))


def kernel(x_prompt, x_sample, cache_a_k, cache_a_v, cache_b_k, cache_b_v, cache_c_kv, cache_c_kpe, c, c_ctx,
           w_mod, b_mod, ln_g, ln_b, ffn_w1, ffn_w3, ffn_w2, w_in, w_o, a_lambda, a_subln_g, b_sink,
           c_q_norm_g, c_w_q_up, c_kv_norm_g, c_w_kv_up):
    n_req, seq, _ = x_prompt.shape
    n_dec, n_lat, _ = x_sample.shape
    n_ctx = cache_a_k.shape[2]
    assert n_lat % BLOCK == 0 and n_lat % TOKEN_TILE == 0 and (n_req * seq) % TOKEN_TILE == 0
    assert 1 + n_dec <= MOD_ROWS

    cond8 = jnp.concatenate([c_ctx[None, :], c, jnp.zeros((MOD_ROWS - 1 - n_dec, D_MODEL), F32)], axis=0)
    mod = _modulation(cond8, w_mod, b_mod[:, None, :]).reshape(DEPTH, MOD_ROWS, N_MOD, 1, D_MODEL)

    w1 = ffn_w1.astype(BF16)
    w3 = ffn_w3.astype(BF16)
    w2 = ffn_w2.astype(BF16)
    w_in_b = w_in.astype(BF16)
    w_ck = jnp.pad(w_in[:, :, OFF_CKV:], ((0, 0), (0, 0), (0, 256 - C_KV_RANK - C_ROPE))).astype(BF16)
    wo = w_o.astype(BF16)
    g4 = ln_g.reshape(DEPTH, 3, 1, D_MODEL)
    b4 = ln_b.reshape(DEPTH, 3, 1, D_MODEL)
    qg = c_q_norm_g.reshape(DEPTH, 1, C_Q_RANK)
    kvg = c_kv_norm_g.reshape(DEPTH, 1, C_KV_RANK)
    wqu, wc2, wuv = _c_weights(c_w_q_up, c_w_kv_up)
    lam_inits = [0.8 - 0.6 * math.exp(-0.3 * l) for l in range(DEPTH)]
    sg = (jnp.tile(a_subln_g, (1, A_HEADS)) * (1.0 - np.asarray(lam_inits, np.float32))[:, None])[:, None, :]
    sg_t = jnp.broadcast_to(sg[:, 0, :, None], (DEPTH, A_HEADS * A_DV, seq))
    wuv_t = jnp.swapaxes(wuv, 1, 2)
    ctx_pad = jnp.zeros(cache_c_kv.shape[:3] + (256 - C_KV_RANK - C_ROPE,), F32)
    ctx = (cache_a_k.reshape(n_dec, DEPTH, n_ctx, -1),
           jnp.swapaxes(cache_a_v.reshape(n_dec, DEPTH, n_ctx, -1), 2, 3),
           _rep_kv(cache_b_k), _rep_kv(cache_b_v),
           jnp.concatenate([cache_c_kv, cache_c_kpe, ctx_pad], axis=-1),
           jnp.swapaxes(cache_c_kv, 2, 3))
    ctx = tuple(t.astype(BF16) for t in ctx)
    sg_t_lat = jnp.broadcast_to(sg[:, 0, :, None], (DEPTH, A_HEADS * A_DV, BLOCK))
    tabs = _rope_tables(n_lat // GRID_W, A_DQK) + _rope_tables(n_lat // GRID_W, B_DH)
    tiles_per_lat = n_lat // TOKEN_TILE

    def prompt_row(i):
        return 0

    def sample_row(i):
        return 1 + i // tiles_per_lat

    xp = x_prompt.reshape(n_req * seq, D_MODEL)
    xs = x_sample.reshape(n_dec * n_lat, D_MODEL)
    states = []
    for l in range(DEPTH):
        proj_w = (w_in_b, w_ck, qg, wqu, wc2, kvg)
        attn_w = (wo, g4, b4, a_lambda, sg_t_lat, b_sink, wuv_t, lam_inits[l])
        attn_w_ctx = (wo, g4, b4, a_lambda, sg_t, b_sink, wuv_t, lam_inits[l])

        xp = _ffn(xp, mod, l, 0, w1, w3, w2, g4, b4, prompt_row)
        outs = _project(xp, mod, l, *proj_w, prompt_row, with_state=True)
        states.append(outs[8:])
        xp = _attn_ctx(outs[:8], xp, mod, l, *attn_w_ctx, seq)
        xp = _ffn(xp, mod, l, 1, w1, w3, w2, g4, b4, prompt_row)

        xs = _ffn(xs, mod, l, 0, w1, w3, w2, g4, b4, sample_row)
        outs = _project(xs, mod, l, *proj_w, sample_row, tabs=tabs)
        xs = _attn_lat(outs, ctx, xs, mod, l, *attn_w, n_dec, n_lat)
        xs = _ffn(xs, mod, l, 1, w1, w3, w2, g4, b4, sample_row)

    def stacked(k, tail):
        return jnp.stack([states[l][k].reshape((n_req, seq) + tail) for l in range(DEPTH)], axis=1)

    return (xp.reshape(n_req, seq, D_MODEL), xs.reshape(n_dec, n_lat, D_MODEL),
            stacked(0, (A_HEADS, 2 * A_DQK)), stacked(1, (A_HEADS, A_DV)),
            stacked(2, (B_KV_HEADS, B_DH)), stacked(3, (B_KV_HEADS, B_DH)),
            stacked(4, (C_KV_RANK,)), stacked(5, (C_ROPE,)))
```

```python
import functools
import math

import jax
import jax.numpy as jnp
import numpy as np
from jax import lax
from jax.experimental import pallas as pl
from jax.experimental.pallas import tpu as pltpu

D_MODEL = 1024
DEPTH = 2
GRID_W = 64
BLOCK = 128
A_HEADS = 4
A_DQK = 32
A_DV = 64
B_HEADS = 8
B_KV_HEADS = 2
B_GROUP = B_HEADS // B_KV_HEADS
B_DH = 64
WINDOW = 128
C_HEADS = 4
C_Q_RANK = 256
C_KV_RANK = 128
C_NOPE = 64
C_ROPE = 32
C_DV = 64
D_FF = 2816
N_MOD = 9
ROPE_BASE = 10000.0
LN_EPS = 1e-5
RMS_EPS = 1e-6
DN_ALPHA = (2 * DEPTH) ** 0.25
FFN_RES = 0.5
LOG2E = 1.4426950408889634

LANES = 128
MXU_DIM = 256
FF_CHUNK = MXU_DIM
N_FF_CHUNKS = D_FF // FF_CHUNK
TOKEN_TILE = 512
MOD_ROWS = 8
MOD_COL_TILE = 1024
KEY_CHUNK = 512
VMEM_LIMIT = 56 * 1024 * 1024
NEG_BIG = -1e30

OFF_AQ = 0
OFF_AK = 256
OFF_AV = 512
OFF_BQ = 768
OFF_BKV = 1280
OFF_CQD = 1536
OFF_CKV = 1792

BF16 = jnp.bfloat16
F32 = jnp.float32


def _dot(a, b):
    return jnp.dot(a, b, preferred_element_type=F32)


def _dot_nt(a, b):
    return lax.dot_general(a, b, (((1,), (1,)), ((), ())), preferred_element_type=F32)


def _cparams(n_grid):
    return pltpu.CompilerParams(dimension_semantics=("arbitrary",) * n_grid, vmem_limit_bytes=VMEM_LIMIT)


def _pick(arr, *lead):
    tail = arr.shape[len(lead):]
    zeros = (0,) * len(tail)
    return pl.BlockSpec((None,) * len(lead) + tail, lambda *_: lead + zeros, pipeline_mode=pl.Buffered(1))


def _mod_spec(layer, slot, row_fn):
    return pl.BlockSpec((None, None, None, 1, D_MODEL), lambda i, *_: (layer, row_fn(i), slot, 0, 0))


def _layernorm(y, g, b):
    mu = jnp.mean(y, axis=-1, keepdims=True)
    d = y - mu
    var = jnp.mean(d * d, axis=-1, keepdims=True)
    return d * lax.rsqrt(var + LN_EPS) * g + b


def _mod_kernel(c_ref, w_ref, b_ref, o_ref):
    c = c_ref[...]
    s = (c / (1.0 + jnp.exp(-c))).astype(BF16)
    o_ref[...] = _dot(s, w_ref[...].astype(BF16)) + b_ref[...]


def _modulation(cond8, w_mod, b_mod):
    n_col = N_MOD * D_MODEL
    return pl.pallas_call(
        _mod_kernel,
        grid=(DEPTH, n_col // MOD_COL_TILE),
        in_specs=[
            pl.BlockSpec((MOD_ROWS, D_MODEL), lambda l, j: (0, 0)),
            pl.BlockSpec((None, D_MODEL, MOD_COL_TILE), lambda l, j: (l, 0, j)),
            pl.BlockSpec((None, 1, MOD_COL_TILE), lambda l, j: (l, 0, j)),
        ],
        out_specs=pl.BlockSpec((None, MOD_ROWS, MOD_COL_TILE), lambda l, j: (l, 0, j)),
        out_shape=jax.ShapeDtypeStruct((DEPTH, MOD_ROWS, n_col), F32),
        compiler_params=_cparams(2),
        name="modulation",
    )(cond8, w_mod, b_mod)


def _ffn_kernel(x_ref, sh_ref, sc_ref, gt_ref, w1_ref, w3_ref, w2_ref, g_ref, b_ref, o_ref, h_scr):
    x = x_ref[...]
    h_scr[...] = (x * (1.0 + sc_ref[...]) + sh_ref[...]).astype(BF16)
    acc = None
    for c in range(N_FF_CHUNKS):
        cs = slice(c * FF_CHUNK, (c + 1) * FF_CHUNK)
        h = h_scr[...]
        a = _dot(h, w1_ref[:, cs])
        b = _dot(h, w3_ref[:, cs])
        u = (a / (1.0 + jnp.exp(-a))) * b
        d = _dot(u.astype(BF16), w2_ref[cs, :])
        acc = d if acc is None else acc + d
    y = DN_ALPHA * x + FFN_RES * gt_ref[...] * acc
    o_ref[...] = _layernorm(y, g_ref[...], b_ref[...])


def _ffn(x, mod, layer, half, w1, w3, w2, ln_g, ln_b, row_fn):
    n = x.shape[0]
    slot = 2 * half
    tile = pl.BlockSpec((TOKEN_TILE, D_MODEL), lambda i: (i, 0))
    return pl.pallas_call(
        _ffn_kernel,
        grid=(n // TOKEN_TILE,),
        in_specs=[
            tile,
            _mod_spec(layer, 3 * slot, row_fn), _mod_spec(layer, 3 * slot + 1, row_fn),
            _mod_spec(layer, 3 * slot + 2, row_fn),
            _pick(w1, layer, half), _pick(w3, layer, half), _pick(w2, layer, half),
            _pick(ln_g, layer, slot), _pick(ln_b, layer, slot),
        ],
        out_specs=tile,
        out_shape=jax.ShapeDtypeStruct((n, D_MODEL), F32),
        scratch_shapes=[pltpu.VMEM((TOKEN_TILE, D_MODEL), BF16)],
        compiler_params=_cparams(1),
        name="ffn_half",
    )(x, mod, mod, mod, w1, w3, w2, ln_g, ln_b)


def _rope(x, cos, sin_lo, sin_hi, q):
    return x * cos + pltpu.roll(x, LANES - q, 1) * sin_lo + pltpu.roll(x, q, 1) * sin_hi


def _rope_cols(x, tabs, q):
    cos, sin_lo, sin_hi = tabs
    parts = [_rope(x[:, j * LANES:(j + 1) * LANES], cos, sin_lo, sin_hi, q) for j in range(x.shape[1] // LANES)]
    return parts[0] if len(parts) == 1 else jnp.concatenate(parts, axis=1)


def _rms(x, g):
    return x * lax.rsqrt(jnp.mean(x * x, axis=-1, keepdims=True) + RMS_EPS) * g


def _repeat_kv_heads(t):
    lane = lax.broadcasted_iota(jnp.int32, (1, LANES), 1)
    first = lane < B_DH
    swapped = pltpu.roll(t, B_DH, 1)
    k0 = jnp.where(first, t, swapped)
    k1 = jnp.where(first, swapped, t)
    return jnp.concatenate([k0, k0, k1, k1], axis=1)


def _project_kernel(*refs, with_rope, with_state, n_carried, layer):
    (x_ref, sh_ref, sc_ref, win_ref, wck_ref, qg_ref, wqu_ref, wc2_ref, kvg_ref), rest = refs[:9], refs[9:]
    if with_rope:
        tab_refs, rest = rest[:6], rest[6:]
    rest = rest[n_carried:]
    (qa_ref, ka_ref, va_ref, qb_ref, kb_ref, vb_ref, qc_ref, kc_ref), rest = rest[:8], rest[8:]
    h_scr = rest[-1]
    rest = rest[:-1]

    h_scr[...] = (x_ref[...] * (1.0 + sc_ref[...]) + sh_ref[...]).astype(BF16)

    def piece(off, width):
        return _dot(h_scr[...], win_ref[:, off:off + width])

    a_q = piece(OFF_AQ, 256)
    a_k = piece(OFF_AK, 256)
    a_v = piece(OFF_AV, 256)
    b_q = piece(OFF_BQ, 512)
    b_kv = piece(OFF_BKV, 256)
    b_k = b_kv[:, :LANES]
    b_v = b_kv[:, LANES:]
    c_qd = piece(OFF_CQD, 256)
    c_k = _dot(h_scr[...], wck_ref[...])
    c_kvd = c_k[:, :LANES]
    c_kpe = c_k[:, LANES:]

    c_q = _dot(_rms(c_qd, qg_ref[...]).astype(BF16), wqu_ref[...])
    c_qn = c_q[:, :256]
    c_qp = c_q[:, 256:384]
    c_kv = _rms(c_kvd, kvg_ref[...])

    if with_state:
        for ref, val in zip(rest, (a_k, a_v, b_k, b_v, c_kv, c_kpe[:, :C_ROPE])):
            if len(ref.shape) == 4:
                slab = ref.shape[:1] + ref.shape[2:]
                for d in range(ref.shape[1]):
                    ref[:, d] = val.reshape(slab) if d == layer else jnp.zeros(slab, F32)
            else:
                ref[...] = val.reshape(ref.shape)

    if with_rope:
        ta = tuple(r[...] for r in tab_refs[:3])
        tb = tuple(r[...] for r in tab_refs[3:])
        lane = lax.broadcasted_iota(jnp.int32, (1, LANES), 1)
        keep = (lane < C_ROPE).astype(F32)
        tck = tuple(v * keep for v in ta)
        a_q = _rope_cols(a_q, ta, A_DQK // 4)
        a_k = _rope_cols(a_k, ta, A_DQK // 4)
        b_q = _rope_cols(b_q, tb, B_DH // 4)
        b_k = _rope_cols(b_k, tb, B_DH // 4)
        c_qp = _rope_cols(c_qp, ta, C_ROPE // 4)
        c_kpe = _rope_cols(c_kpe, tck, C_ROPE // 4)

    qa_ref[...] = (a_q * (A_DQK ** -0.5 * LOG2E)).astype(BF16)
    ka_ref[...] = a_k.astype(BF16)
    va_ref[...] = a_v.astype(BF16)
    qb_ref[...] = (b_q * (B_DH ** -0.5 * LOG2E)).astype(BF16)
    kb_ref[...] = _repeat_kv_heads(b_k).astype(BF16)
    vb_ref[...] = _repeat_kv_heads(b_v).astype(BF16)
    c_scale = (C_NOPE + C_ROPE) ** -0.5 * LOG2E
    c_qcat = jnp.concatenate([c_qn * c_scale, c_qp * c_scale], axis=1).astype(BF16)
    qc_ref[...] = _dot(c_qcat, wc2_ref[...]).astype(BF16)
    kc_ref[...] = jnp.concatenate([c_kv, c_kpe], axis=1).astype(BF16)


def _project(x, mod, layer, w_in, w_ck, qg, wqu, wc2, kvg, row_fn, tabs=None, state_seq=None, carried=()):
    n = x.shape[0]
    with_rope = tabs is not None
    with_state = state_seq is not None

    def rows(width):
        return pl.BlockSpec((TOKEN_TILE, width), lambda i: (i, 0))

    in_specs = [rows(D_MODEL), _mod_spec(layer, 3, row_fn), _mod_spec(layer, 4, row_fn),
                _pick(w_in, layer), _pick(w_ck, layer), _pick(qg, layer), _pick(wqu, layer),
                _pick(wc2, layer), _pick(kvg, layer)]
    args = [x, mod, mod, w_in, w_ck, qg, wqu, wc2, kvg]
    if with_rope:
        tiles_per_seq = tabs[0].shape[0] // TOKEN_TILE
        in_specs += [pl.BlockSpec((TOKEN_TILE, LANES), lambda i: (i % tiles_per_seq, 0))] * 6
        args += list(tabs)
    widths = [256, 256, 256, 512, 512, 512, 1024, 256]
    out_specs = [rows(w) for w in widths]
    out_shape = [jax.ShapeDtypeStruct((n, w), BF16) for w in widths]
    aliases = {}
    if with_state:
        per_tile = TOKEN_TILE // state_seq
        for k, w in enumerate([256, 256, 128, 128, 128, C_ROPE]):
            if carried:
                out_specs.append(pl.BlockSpec((per_tile, None, state_seq, w), lambda i: (i, layer, 0, 0)))
            else:
                out_specs.append(pl.BlockSpec((per_tile, DEPTH, state_seq, w), lambda i: (i, 0, 0, 0)))
            out_shape.append(jax.ShapeDtypeStruct((n // state_seq, DEPTH, state_seq, w), F32))
        for k, arr in enumerate(carried):
            aliases[len(args)] = len(widths) + k
            in_specs.append(pl.BlockSpec(memory_space=pl.ANY))
            args.append(arr)
    return pl.pallas_call(
        functools.partial(_project_kernel, with_rope=with_rope, with_state=with_state, n_carried=len(carried),
                          layer=layer),
        grid=(n // TOKEN_TILE,),
        in_specs=in_specs,
        out_specs=out_specs,
        out_shape=out_shape,
        input_output_aliases=aliases,
        scratch_shapes=[pltpu.VMEM((TOKEN_TILE, D_MODEL), BF16)],
        compiler_params=_cparams(1),
        name="project_latent" if with_rope else "project_context",
    )(*args)


def _lane_mask(width, lo, size, dtype):
    lane = lax.broadcasted_iota(jnp.int32, (1, width), 1)
    return ((lane >= lo) & (lane < lo + size)).astype(dtype)


def _fold_lanes(x, op):
    r = x[:, :LANES]
    for j in range(1, x.shape[1] // LANES):
        r = op(r, x[:, j * LANES:(j + 1) * LANES])
    return r


def _diff_heads(acc, lam, sg_ref, tq):
    o_a = jnp.zeros((tq, 256), F32)
    for h in range(A_HEADS):
        o_h = acc[2 * h * tq:(2 * h + 1) * tq] - lam * acc[(2 * h + 1) * tq:(2 * h + 2) * tq]
        om = o_h * _lane_mask(256, h * A_DV, A_DV, F32)
        ms = jnp.sum(om * om, axis=-1, keepdims=True) * (1.0 / A_DV)
        o_a = o_a + om * lax.rsqrt(ms + RMS_EPS)
    return o_a * sg_ref[...]


def _finish(o_parts, x_ref, gt_ref, wo_ref, g_ref, b_ref, o_ref):
    o = jnp.concatenate(o_parts, axis=1).astype(BF16)
    f = _dot(o, wo_ref[...])
    y = DN_ALPHA * x_ref[...] + gt_ref[...] * f
    o_ref[...] = _layernorm(y, g_ref[...], b_ref[...])


def _diff_lambda(lp_ref, lam_init):
    lp = lp_ref[...]
    s1 = jnp.sum(lp[0:1, :] * lp[1:2, :], axis=-1, keepdims=True)
    s2 = jnp.sum(lp[2:3, :] * lp[3:4, :], axis=-1, keepdims=True)
    return jnp.exp(s1) - jnp.exp(s2) + lam_init


def _diff_head_t(blk, lam, tq):
    o_h = blk[:, :tq] - lam * blk[:, tq:]
    ms = jnp.mean(o_h * o_h, axis=0, keepdims=True)
    return o_h * lax.rsqrt(ms + RMS_EPS)


def _group_heads_t(o_t, tq):
    return jnp.concatenate([o_t[B_DH * g:B_DH * (g + 1), g * tq:(g + 1) * tq] for g in range(B_GROUP)], axis=0)


def _sink_row(sink_ref, layer, kv_head, tq):
    return jnp.concatenate([jnp.full((1, tq), sink_ref[layer, kv_head * B_GROUP + g] * LOG2E, F32)
                            for g in range(B_GROUP)], axis=1)


def _finish_t(parts, x_ref, gt_ref, wo_ref, g_ref, b_ref, o_ref):
    _finish([jnp.concatenate(parts, axis=0).T], x_ref, gt_ref, wo_ref, g_ref, b_ref, o_ref)


def _attn_ctx_kernel(qa_ref, ka_ref, va_ref, qb_ref, kb_ref, vb_ref, qc_ref, kc_ref,
                     x_ref, gt_ref, wo_ref, g_ref, b_ref, lp_ref, sg_ref, sink_ref, wuvt_ref,
                     o_ref, *, lam_init, layer):
    tq = qa_ref.shape[0]
    lam = _diff_lambda(lp_ref, lam_init)

    def softmax_t(s_t, sink=None):
        m = jnp.max(s_t, axis=0, keepdims=True)
        if sink is not None:
            m = jnp.maximum(m, sink)
        e = jnp.exp2(s_t - m)
        l = jnp.sum(e, axis=0, keepdims=True)
        if sink is not None:
            l = l + jnp.exp2(sink - m)
        return e.astype(BF16), l

    def values_t(v, e, l):
        return _dot(v.T, e) / l

    qa = qa_ref[...]
    q8 = jnp.concatenate([qa * _lane_mask(256, hj * A_DQK, A_DQK, BF16) for hj in range(2 * A_HEADS)], axis=0)
    e, l = softmax_t(_dot_nt(ka_ref[...], q8))
    o_t = values_t(va_ref[...], e, l)
    heads = [_diff_head_t(o_t[A_DV * h:A_DV * (h + 1), 2 * h * tq:(2 * h + 2) * tq], lam, tq)
             for h in range(A_HEADS)]
    parts = [jnp.concatenate(heads, axis=0) * sg_ref[...]]

    qb = qb_ref[...]
    kb = kb_ref[...]
    vb = vb_ref[...]
    for c in range(B_KV_HEADS):
        cs = slice(256 * c, 256 * c + 256)
        q4 = jnp.concatenate([qb[:, cs] * _lane_mask(256, g * B_DH, B_DH, BF16) for g in range(B_GROUP)], axis=0)
        e, den = softmax_t(_dot_nt(kb[:, cs], q4), _sink_row(sink_ref, layer, c, tq))
        parts.append(_group_heads_t(values_t(vb[:, cs], e, den), tq))

    qc = qc_ref[...]
    kc = kc_ref[...]
    q4 = jnp.concatenate([qc[:, 256 * h:256 * h + 256] for h in range(C_HEADS)], axis=0)
    e, l = softmax_t(_dot_nt(kc, q4))
    o_t = values_t(kc, e, l)
    o_lat_t = jnp.concatenate([o_t[:C_KV_RANK, h * tq:(h + 1) * tq] for h in range(C_HEADS)], axis=0)
    parts.append(_dot(wuvt_ref[...], o_lat_t.astype(BF16)))
    _finish_t(parts, x_ref, gt_ref, wo_ref, g_ref, b_ref, o_ref)


def _attn_ctx(proj, x, mod, layer, wo, ln_g, ln_b, lp, sg, sink, wuv, lam_init, seq):
    qa, ka, va, qb, kb, vb, qc, kc = proj
    n = x.shape[0]

    def rows(width):
        return pl.BlockSpec((seq, width), lambda i: (i, 0))

    return pl.pallas_call(
        functools.partial(_attn_ctx_kernel, lam_init=lam_init, layer=layer),
        grid=(n // seq,),
        in_specs=[rows(256), rows(256), rows(256), rows(512), rows(512), rows(512), rows(1024), rows(256),
                  rows(D_MODEL), _mod_spec(layer, 5, lambda i: 0),
                  _pick(wo, layer), _pick(ln_g, layer, 1), _pick(ln_b, layer, 1), _pick(lp, layer),
                  _pick(sg, layer), pl.BlockSpec(memory_space=pltpu.SMEM), _pick(wuv, layer)],
        out_specs=rows(D_MODEL),
        out_shape=jax.ShapeDtypeStruct((n, D_MODEL), F32),
        compiler_params=_cparams(1),
        name="attention_context",
    )(qa, ka, va, qb, kb, vb, qc, kc, x, mod, wo, ln_g, ln_b, lp, sg, sink, wuv)


def _attn_lat_kernel(qa_ref, qb_ref, qc_ref,
                     ka_ref, va_ref, kb_ref, vb_ref, kc_ref,
                     xka_ref, xva_ref, xkb_ref, xvb_ref, xkc_ref,
                     x_ref, gt_ref, wo_ref, g_ref, b_ref, lp_ref, sg_ref, sink_ref, wuv_ref,
                     o_ref, qa_scr, qc_scr, sa_scr, sc_scr, *, lam_init, layer, n_lat):
    tq = qa_ref.shape[0]
    n_ctx = xka_ref.shape[0]
    n_chunks = n_lat // KEY_CHUNK
    lam = _diff_lambda(lp_ref, lam_init)

    def key_span(c):
        return slice(n_ctx + c * KEY_CHUNK, n_ctx + (c + 1) * KEY_CHUNK)

    def key_rows(c):
        return slice(c * KEY_CHUNK, (c + 1) * KEY_CHUNK)

    def score_pass(q_scr, s_scr, k_ref, xk_ref):
        s = _dot_nt(q_scr[...], xk_ref[...])
        s_scr[:, :n_ctx] = s
        m_run = _fold_lanes(s, jnp.maximum)
        for c in range(n_chunks):
            s = _dot_nt(q_scr[...], k_ref[key_rows(c), :])
            s_scr[:, key_span(c)] = s
            m_run = jnp.maximum(m_run, _fold_lanes(s, jnp.maximum))
        return jnp.max(m_run, axis=-1, keepdims=True)

    def value_pass(s_scr, m, v_ref, xv_ref):
        e = jnp.exp2(s_scr[:, :n_ctx] - m)
        l_run = _fold_lanes(e, jnp.add)
        acc = _dot(e.astype(BF16), xv_ref[...])
        for c in range(n_chunks):
            e = jnp.exp2(s_scr[:, key_span(c)] - m)
            l_run = l_run + _fold_lanes(e, jnp.add)
            acc = acc + _dot(e.astype(BF16), v_ref[key_rows(c), :])
        return acc / jnp.sum(l_run, axis=-1, keepdims=True)

    qa = qa_ref[...]
    for hj in range(2 * A_HEADS):
        qa_scr[hj * tq:(hj + 1) * tq, :] = qa * _lane_mask(256, hj * A_DQK, A_DQK, BF16)
    for h in range(C_HEADS):
        qc_scr[h * tq:(h + 1) * tq, :] = qc_ref[:, 256 * h:256 * (h + 1)]
    m_a = score_pass(qa_scr, sa_scr, ka_ref, xka_ref)
    m_c = score_pass(qc_scr, sc_scr, kc_ref, xkc_ref)

    o_a = _diff_heads(value_pass(sa_scr, m_a, va_ref, xva_ref), lam, sg_ref, tq)

    i = pl.program_id(1)
    q0 = i * tq
    win = tq + 2 * WINDOW
    start = pl.multiple_of(jnp.clip(q0 - WINDOW, 0, n_lat - win), BLOCK)
    kpos = start + lax.broadcasted_iota(jnp.int32, (1, win), 1)
    qpos = q0 + (lax.broadcasted_iota(jnp.int32, (B_GROUP * tq, 1), 0) & (tq - 1))
    valid = jnp.abs(kpos - qpos) <= WINDOW
    qb = qb_ref[...]
    o_b = []
    for c in range(B_KV_HEADS):
        cs = slice(256 * c, 256 * c + 256)
        masks = [_lane_mask(256, g * B_DH, B_DH, BF16) for g in range(B_GROUP)]
        q4 = jnp.concatenate([qb[:, cs] * mk for mk in masks], axis=0)
        sink = jnp.concatenate([jnp.full((tq, 1), sink_ref[layer, c * B_GROUP + g] * LOG2E, F32)
                                for g in range(B_GROUP)], axis=0)
        s_ctx = _dot_nt(q4, xkb_ref[:, cs])
        s_loc = jnp.where(valid, _dot_nt(q4, kb_ref[pl.ds(start, win), cs]), NEG_BIG)
        m_fold = jnp.maximum(_fold_lanes(s_ctx, jnp.maximum), _fold_lanes(s_loc, jnp.maximum))
        m = jnp.maximum(jnp.max(m_fold, axis=-1, keepdims=True), sink)
        e_ctx = jnp.exp2(s_ctx - m)
        e_loc = jnp.exp2(s_loc - m)
        l_fold = _fold_lanes(e_ctx, jnp.add) + _fold_lanes(e_loc, jnp.add)
        den = jnp.sum(l_fold, axis=-1, keepdims=True) + jnp.exp2(sink - m)
        acc = (_dot(e_ctx.astype(BF16), xvb_ref[:, cs])
               + _dot(e_loc.astype(BF16), vb_ref[pl.ds(start, win), cs])) / den
        oc = acc[:tq] * masks[0].astype(F32)
        for g in range(1, B_GROUP):
            oc = oc + acc[g * tq:(g + 1) * tq] * masks[g].astype(F32)
        o_b.append(oc)

    acc = value_pass(sc_scr, m_c, kc_ref, xkc_ref)
    o_lat = jnp.concatenate([acc[h * tq:(h + 1) * tq, :C_KV_RANK] for h in range(C_HEADS)], axis=1)
    o_c = _dot(o_lat.astype(BF16), wuv_ref[...])

    _finish([o_a] + o_b + [o_c], x_ref, gt_ref, wo_ref, g_ref, b_ref, o_ref)


def _attn_lat(proj, ctx, x, mod, layer, wo, ln_g, ln_b, lp, sg, sink, wuv, lam_init, n_batch, n_lat):
    qa, ka, va, qb, kb, vb, qc, kc = proj
    xka, xva, xkb, xvb, xkc = ctx
    tq = BLOCK
    tiles = n_lat // tq
    n_ctx = xka.shape[2]

    def qrows(width):
        return pl.BlockSpec((tq, width), lambda bi, i: (bi * tiles + i, 0))

    def seq(width):
        return pl.BlockSpec((n_lat, width), lambda bi, i: (bi, 0), pipeline_mode=pl.Buffered(1))

    def cseq(width):
        return pl.BlockSpec((None, None, n_ctx, width), lambda bi, i: (bi, layer, 0, 0))

    return pl.pallas_call(
        functools.partial(_attn_lat_kernel, lam_init=lam_init, layer=layer, n_lat=n_lat),
        grid=(n_batch, tiles),
        in_specs=[qrows(256), qrows(512), qrows(1024),
                  seq(256), seq(256), seq(512), seq(512), seq(256),
                  cseq(256), cseq(256), cseq(512), cseq(512), cseq(256),
                  qrows(D_MODEL),
                  pl.BlockSpec((None, None, None, 1, D_MODEL), lambda bi, i: (layer, 1 + bi, 5, 0, 0)),
                  _pick(wo, layer), _pick(ln_g, layer, 1), _pick(ln_b, layer, 1), _pick(lp, layer),
                  _pick(sg, layer), pl.BlockSpec(memory_space=pltpu.SMEM), _pick(wuv, layer)],
        out_specs=qrows(D_MODEL),
        out_shape=jax.ShapeDtypeStruct((n_batch * n_lat, D_MODEL), F32),
        scratch_shapes=[pltpu.VMEM((2 * A_HEADS * tq, 256), BF16),
                        pltpu.VMEM((C_HEADS * tq, 256), BF16),
                        pltpu.VMEM((2 * A_HEADS * tq, n_ctx + n_lat), F32),
                        pltpu.VMEM((C_HEADS * tq, n_ctx + n_lat), F32)],
        compiler_params=_cparams(2),
        name="attention_latent",
    )(qa, qb, qc, ka, va, kb, vb, kc, xka, xva, xkb, xvb, xkc, x, mod, wo, ln_g, ln_b, lp, sg, sink, wuv)


def _rope_tables(rows, dim):
    row = jnp.repeat(jnp.arange(rows, dtype=F32), GRID_W)
    col = jnp.tile(jnp.arange(GRID_W, dtype=F32), rows)
    a = dim // 2
    inv = jnp.power(ROPE_BASE, -jnp.arange(0, a, 2, dtype=F32) / a)
    ar = row[:, None] * inv[None, :]
    ac = col[:, None] * inv[None, :]
    ang = jnp.concatenate([ar, ar, ac, ac], axis=-1)
    reps = LANES // dim
    cos = jnp.tile(jnp.cos(ang), (1, reps))
    sin = jnp.tile(jnp.sin(ang), (1, reps))
    lower = (np.arange(LANES) % (dim // 2)) < (dim // 4)
    sin_lo = jnp.where(lower[None, :], -sin, 0.0)
    sin_hi = jnp.where(lower[None, :], 0.0, sin)
    return cos, sin_lo, sin_hi


def _c_weights(w_q_up, w_kv_up):
    n_l = w_q_up.shape[0]
    wq = w_q_up.reshape(n_l, C_Q_RANK, C_HEADS, C_NOPE + C_ROPE)
    wqu = jnp.concatenate([wq[..., :C_NOPE].reshape(n_l, C_Q_RANK, -1),
                           wq[..., C_NOPE:].reshape(n_l, C_Q_RANK, -1)], axis=-1)
    eye_h = np.eye(C_HEADS, dtype=np.float32)
    w_uk = w_kv_up[..., :C_NOPE]
    w_uv = w_kv_up[..., C_NOPE:]
    top = jnp.einsum('lchd,hg->lhdgc', w_uk, eye_h)
    top = jnp.pad(top, ((0, 0),) * 4 + ((0, 256 - C_KV_RANK),)).reshape(n_l, C_HEADS * C_NOPE, C_HEADS * 256)
    place = np.zeros((C_HEADS, C_ROPE, C_HEADS, 256), np.float32)
    for h in range(C_HEADS):
        place[h, np.arange(C_ROPE), h, C_KV_RANK + np.arange(C_ROPE)] = 1.0
    bottom = jnp.broadcast_to(place.reshape(1, C_HEADS * C_ROPE, C_HEADS * 256), (n_l, C_HEADS * C_ROPE, C_HEADS * 256))
    wc2 = jnp.concatenate([top, bottom], axis=1)
    wuv = jnp.einsum('lchd,hg->lhcgd', w_uv, eye_h).reshape(n_l, C_HEADS * C_KV_RANK, C_HEADS * C_DV)
    return wqu.astype(BF16), wc2.astype(BF16), wuv.astype(BF16)


def _rep_kv(t):
    lead = t.shape[:3]
    return jnp.broadcast_to(t[..., None, :], lead + (B_KV_HEADS, B_GROUP, B_DH)).reshape(lead + (-1,))


def kernel(x_prompt, x_sample, cache_a_k, cache_a_v, cache_b_k, cache_b_v, cache_c_kv, cache_c_kpe, c, c_ctx,
           w_mod, b_mod, ln_g, ln_b, ffn_w1, ffn_w3, ffn_w2, w_in, w_o, a_lambda, a_subln_g, b_sink,
           c_q_norm_g, c_w_q_up, c_kv_norm_g, c_w_kv_up):
    n_req, seq, _ = x_prompt.shape
    n_dec, n_lat, _ = x_sample.shape
    n_ctx = cache_a_k.shape[2]
    assert n_lat % BLOCK == 0 and n_lat % TOKEN_TILE == 0 and (n_req * seq) % TOKEN_TILE == 0
    assert 1 + n_dec <= MOD_ROWS

    cond8 = jnp.concatenate([c_ctx[None, :], c, jnp.zeros((MOD_ROWS - 1 - n_dec, D_MODEL), F32)], axis=0)
    mod = _modulation(cond8, w_mod, b_mod[:, None, :]).reshape(DEPTH, MOD_ROWS, N_MOD, 1, D_MODEL)

    w1 = ffn_w1.astype(BF16)
    w3 = ffn_w3.astype(BF16)
    w2 = ffn_w2.astype(BF16)
    w_in_b = w_in.astype(BF16)
    w_ck = jnp.pad(w_in[:, :, OFF_CKV:], ((0, 0), (0, 0), (0, 256 - C_KV_RANK - C_ROPE))).astype(BF16)
    wo = w_o.astype(BF16)
    g4 = ln_g.reshape(DEPTH, 3, 1, D_MODEL)
    b4 = ln_b.reshape(DEPTH, 3, 1, D_MODEL)
    qg = c_q_norm_g.reshape(DEPTH, 1, C_Q_RANK)
    kvg = c_kv_norm_g.reshape(DEPTH, 1, C_KV_RANK)
    wqu, wc2, wuv = _c_weights(c_w_q_up, c_w_kv_up)
    lam_inits = [0.8 - 0.6 * math.exp(-0.3 * l) for l in range(DEPTH)]
    sg = (jnp.tile(a_subln_g, (1, A_HEADS)) * (1.0 - np.asarray(lam_inits, np.float32))[:, None])[:, None, :]
    sg_t = jnp.broadcast_to(sg[:, 0, :, None], (DEPTH, A_HEADS * A_DV, seq))
    wuv_t = jnp.swapaxes(wuv, 1, 2)
    ctx_pad = jnp.zeros(cache_c_kv.shape[:3] + (256 - C_KV_RANK - C_ROPE,), F32)
    ctx = (cache_a_k.reshape(n_dec, DEPTH, n_ctx, -1), cache_a_v.reshape(n_dec, DEPTH, n_ctx, -1),
           _rep_kv(cache_b_k), _rep_kv(cache_b_v),
           jnp.concatenate([cache_c_kv, cache_c_kpe, ctx_pad], axis=-1))
    ctx = tuple(t.astype(BF16) for t in ctx)
    tabs = _rope_tables(n_lat // GRID_W, A_DQK) + _rope_tables(n_lat // GRID_W, B_DH)
    tiles_per_lat = n_lat // TOKEN_TILE

    def prompt_row(i):
        return 0

    def sample_row(i):
        return 1 + i // tiles_per_lat

    xp = x_prompt.reshape(n_req * seq, D_MODEL)
    xs = x_sample.reshape(n_dec * n_lat, D_MODEL)
    states = ()
    for l in range(DEPTH):
        proj_w = (w_in_b, w_ck, qg, wqu, wc2, kvg)
        attn_w = (wo, g4, b4, a_lambda, sg, b_sink, wuv, lam_inits[l])
        attn_w_ctx = (wo, g4, b4, a_lambda, sg_t, b_sink, wuv_t, lam_inits[l])

        xp = _ffn(xp, mod, l, 0, w1, w3, w2, g4, b4, prompt_row)
        outs = _project(xp, mod, l, *proj_w, prompt_row, state_seq=seq, carried=states)
        states = tuple(outs[8:])
        xp = _attn_ctx(outs[:8], xp, mod, l, *attn_w_ctx, seq)
        xp = _ffn(xp, mod, l, 1, w1, w3, w2, g4, b4, prompt_row)

        xs = _ffn(xs, mod, l, 0, w1, w3, w2, g4, b4, sample_row)
        outs = _project(xs, mod, l, *proj_w, sample_row, tabs=tabs)
        xs = _attn_lat(outs, ctx, xs, mod, l, *attn_w, n_dec, n_lat)
        xs = _ffn(xs, mod, l, 1, w1, w3, w2, g4, b4, sample_row)

    def state(k, tail):
        return states[k].reshape((n_req, DEPTH, seq) + tail)

    return (xp.reshape(n_req, seq, D_MODEL), xs.reshape(n_dec, n_lat, D_MODEL),
            state(0, (A_HEADS, 2 * A_DQK)), state(1, (A_HEADS, A_DV)),
            state(2, (B_KV_HEADS, B_DH)), state(3, (B_KV_HEADS, B_DH)),
            state(4, (C_KV_RANK,)), state(5, (C_ROPE,)))
```

```python
import functools
import math

import jax
import jax.numpy as jnp
import numpy as np
from jax import lax
from jax.experimental import pallas as pl
from jax.experimental.pallas import tpu as pltpu

D_MODEL = 1024
DEPTH = 2
GRID_W = 64
BLOCK = 128
A_HEADS = 4
A_DQK = 32
A_DV = 64
B_HEADS = 8
B_KV_HEADS = 2
B_GROUP = B_HEADS // B_KV_HEADS
B_DH = 64
WINDOW = 128
C_HEADS = 4
C_Q_RANK = 256
C_KV_RANK = 128
C_NOPE = 64
C_ROPE = 32
C_DV = 64
D_FF = 2816
N_MOD = 9
ROPE_BASE = 10000.0
LN_EPS = 1e-5
RMS_EPS = 1e-6
DN_ALPHA = (2 * DEPTH) ** 0.25
FFN_RES = 0.5
LOG2E = 1.4426950408889634

LANES = 128
MXU_DIM = 256
FF_CHUNK = MXU_DIM
N_FF_CHUNKS = D_FF // FF_CHUNK
TOKEN_TILE = 1024
MOD_ROWS = 8
MOD_COL_TILE = 1024
KEY_CHUNK = 512
VMEM_LIMIT = 56 * 1024 * 1024
NEG_BIG = -1e30

OFF_AQ = 0
OFF_AK = 256
OFF_AV = 512
OFF_BQ = 768
OFF_BKV = 1280
OFF_CQD = 1536
OFF_CKV = 1792

BF16 = jnp.bfloat16
F32 = jnp.float32


def _dot(a, b):
    return jnp.dot(a, b, preferred_element_type=F32)


def _dot_nt(a, b):
    return lax.dot_general(a, b, (((1,), (1,)), ((), ())), preferred_element_type=F32)


def _cparams(n_grid):
    return pltpu.CompilerParams(dimension_semantics=("arbitrary",) * n_grid, vmem_limit_bytes=VMEM_LIMIT)


def _pick(arr, *lead):
    tail = arr.shape[len(lead):]
    zeros = (0,) * len(tail)
    return pl.BlockSpec((None,) * len(lead) + tail, lambda *_: lead + zeros, pipeline_mode=pl.Buffered(1))


def _mod_spec(layer, slot, row_fn):
    return pl.BlockSpec((None, None, None, 1, D_MODEL), lambda i, *_: (layer, row_fn(i), slot, 0, 0))


def _layernorm(y, g, b):
    mu = jnp.mean(y, axis=-1, keepdims=True)
    d = y - mu
    var = jnp.mean(d * d, axis=-1, keepdims=True)
    return d * lax.rsqrt(var + LN_EPS) * g + b


def _mod_kernel(c_ref, w_ref, b_ref, o_ref):
    c = c_ref[...]
    s = (c / (1.0 + jnp.exp(-c))).astype(BF16)
    o_ref[...] = _dot(s, w_ref[...].astype(BF16)) + b_ref[...]


def _modulation(cond8, w_mod, b_mod):
    n_col = N_MOD * D_MODEL
    return pl.pallas_call(
        _mod_kernel,
        grid=(DEPTH, n_col // MOD_COL_TILE),
        in_specs=[
            pl.BlockSpec((MOD_ROWS, D_MODEL), lambda l, j: (0, 0)),
            pl.BlockSpec((None, D_MODEL, MOD_COL_TILE), lambda l, j: (l, 0, j)),
            pl.BlockSpec((None, 1, MOD_COL_TILE), lambda l, j: (l, 0, j)),
        ],
        out_specs=pl.BlockSpec((None, MOD_ROWS, MOD_COL_TILE), lambda l, j: (l, 0, j)),
        out_shape=jax.ShapeDtypeStruct((DEPTH, MOD_ROWS, n_col), F32),
        compiler_params=_cparams(2),
        name="modulation",
    )(cond8, w_mod, b_mod)


def _ffn_kernel(x_ref, sh_ref, sc_ref, gt_ref, w1_ref, w3_ref, w2_ref, g_ref, b_ref, o_ref, h_scr):
    x = x_ref[...]
    h_scr[...] = (x * (1.0 + sc_ref[...]) + sh_ref[...]).astype(BF16)
    acc = None
    for c in range(N_FF_CHUNKS):
        cs = slice(c * FF_CHUNK, (c + 1) * FF_CHUNK)
        h = h_scr[...]
        a = _dot(h, w1_ref[:, cs])
        b = _dot(h, w3_ref[:, cs])
        u = (a / (1.0 + jnp.exp(-a))) * b
        d = _dot(u.astype(BF16), w2_ref[cs, :])
        acc = d if acc is None else acc + d
    y = DN_ALPHA * x + FFN_RES * gt_ref[...] * acc
    o_ref[...] = _layernorm(y, g_ref[...], b_ref[...])


def _ffn(x, mod, layer, half, w1, w3, w2, ln_g, ln_b, row_fn):
    n = x.shape[0]
    slot = 2 * half
    tile = pl.BlockSpec((TOKEN_TILE, D_MODEL), lambda i: (i, 0))
    return pl.pallas_call(
        _ffn_kernel,
        grid=(n // TOKEN_TILE,),
        in_specs=[
            tile,
            _mod_spec(layer, 3 * slot, row_fn), _mod_spec(layer, 3 * slot + 1, row_fn),
            _mod_spec(layer, 3 * slot + 2, row_fn),
            _pick(w1, layer, half), _pick(w3, layer, half), _pick(w2, layer, half),
            _pick(ln_g, layer, slot), _pick(ln_b, layer, slot),
        ],
        out_specs=tile,
        out_shape=jax.ShapeDtypeStruct((n, D_MODEL), F32),
        scratch_shapes=[pltpu.VMEM((TOKEN_TILE, D_MODEL), BF16)],
        compiler_params=_cparams(1),
        name="ffn_half",
    )(x, mod, mod, mod, w1, w3, w2, ln_g, ln_b)


def _rope(x, cos, sin_lo, sin_hi, q):
    return x * cos + pltpu.roll(x, LANES - q, 1) * sin_lo + pltpu.roll(x, q, 1) * sin_hi


def _rope_cols(x, tabs, q):
    cos, sin_lo, sin_hi = tabs
    parts = [_rope(x[:, j * LANES:(j + 1) * LANES], cos, sin_lo, sin_hi, q) for j in range(x.shape[1] // LANES)]
    return parts[0] if len(parts) == 1 else jnp.concatenate(parts, axis=1)


def _rms(x, g):
    return x * lax.rsqrt(jnp.mean(x * x, axis=-1, keepdims=True) + RMS_EPS) * g


def _repeat_kv_heads(t):
    lane = lax.broadcasted_iota(jnp.int32, (1, LANES), 1)
    first = lane < B_DH
    swapped = pltpu.roll(t, B_DH, 1)
    k0 = jnp.where(first, t, swapped)
    k1 = jnp.where(first, swapped, t)
    return jnp.concatenate([k0, k0, k1, k1], axis=1)


def _project_kernel(*refs, with_rope, with_state, n_carried, layer):
    (x_ref, sh_ref, sc_ref, win_ref, wck_ref, qg_ref, wqu_ref, wc2_ref, kvg_ref), rest = refs[:9], refs[9:]
    if with_rope:
        tab_refs, rest = rest[:6], rest[6:]
    rest = rest[n_carried:]
    (qa_ref, ka_ref, va_ref, qb_ref, kb_ref, vb_ref, qc_ref, kc_ref), rest = rest[:8], rest[8:]
    h_scr = rest[-1]
    rest = rest[:-1]

    h_scr[...] = (x_ref[...] * (1.0 + sc_ref[...]) + sh_ref[...]).astype(BF16)

    def piece(off, width):
        return _dot(h_scr[...], win_ref[:, off:off + width])

    a_q = piece(OFF_AQ, 256)
    a_k = piece(OFF_AK, 256)
    a_v = piece(OFF_AV, 256)
    b_q = piece(OFF_BQ, 512)
    b_kv = piece(OFF_BKV, 256)
    b_k = b_kv[:, :LANES]
    b_v = b_kv[:, LANES:]
    c_qd = piece(OFF_CQD, 256)
    c_k = _dot(h_scr[...], wck_ref[...])
    c_kvd = c_k[:, :LANES]
    c_kpe = c_k[:, LANES:]

    c_q = _dot(_rms(c_qd, qg_ref[...]).astype(BF16), wqu_ref[...])
    c_qn = c_q[:, :256]
    c_qp = c_q[:, 256:384]
    c_kv = _rms(c_kvd, kvg_ref[...])

    if with_state:
        for ref, val in zip(rest, (a_k, a_v, b_k, b_v, c_kv, c_kpe[:, :C_ROPE])):
            if len(ref.shape) == 4:
                slab = ref.shape[:1] + ref.shape[2:]
                for d in range(ref.shape[1]):
                    ref[:, d] = val.reshape(slab) if d == layer else jnp.zeros(slab, F32)
            else:
                ref[...] = val.reshape(ref.shape)

    if with_rope:
        ta = tuple(r[...] for r in tab_refs[:3])
        tb = tuple(r[...] for r in tab_refs[3:])
        lane = lax.broadcasted_iota(jnp.int32, (1, LANES), 1)
        keep = (lane < C_ROPE).astype(F32)
        tck = tuple(v * keep for v in ta)
        a_q = _rope_cols(a_q, ta, A_DQK // 4)
        a_k = _rope_cols(a_k, ta, A_DQK // 4)
        b_q = _rope_cols(b_q, tb, B_DH // 4)
        b_k = _rope_cols(b_k, tb, B_DH // 4)
        c_qp = _rope_cols(c_qp, ta, C_ROPE // 4)
        c_kpe = _rope_cols(c_kpe, tck, C_ROPE // 4)

    qa_ref[...] = (a_q * (A_DQK ** -0.5 * LOG2E)).astype(BF16)
    ka_ref[...] = a_k.astype(BF16)
    va_ref[...] = a_v.astype(BF16)
    qb_ref[...] = (b_q * (B_DH ** -0.5 * LOG2E)).astype(BF16)
    kb_ref[...] = _repeat_kv_heads(b_k).astype(BF16)
    vb_ref[...] = _repeat_kv_heads(b_v).astype(BF16)
    c_scale = (C_NOPE + C_ROPE) ** -0.5 * LOG2E
    c_qcat = jnp.concatenate([c_qn * c_scale, c_qp * c_scale], axis=1).astype(BF16)
    qc_ref[...] = _dot(c_qcat, wc2_ref[...]).astype(BF16)
    kc_ref[...] = jnp.concatenate([c_kv, c_kpe], axis=1).astype(BF16)


def _project(x, mod, layer, w_in, w_ck, qg, wqu, wc2, kvg, row_fn, tabs=None, state_seq=None, carried=()):
    n = x.shape[0]
    with_rope = tabs is not None
    with_state = state_seq is not None

    def rows(width):
        return pl.BlockSpec((TOKEN_TILE, width), lambda i: (i, 0))

    in_specs = [rows(D_MODEL), _mod_spec(layer, 3, row_fn), _mod_spec(layer, 4, row_fn),
                _pick(w_in, layer), _pick(w_ck, layer), _pick(qg, layer), _pick(wqu, layer),
                _pick(wc2, layer), _pick(kvg, layer)]
    args = [x, mod, mod, w_in, w_ck, qg, wqu, wc2, kvg]
    if with_rope:
        tiles_per_seq = tabs[0].shape[0] // TOKEN_TILE
        in_specs += [pl.BlockSpec((TOKEN_TILE, LANES), lambda i: (i % tiles_per_seq, 0))] * 6
        args += list(tabs)
    widths = [256, 256, 256, 512, 512, 512, 1024, 256]
    out_specs = [rows(w) for w in widths]
    out_shape = [jax.ShapeDtypeStruct((n, w), BF16) for w in widths]
    aliases = {}
    if with_state:
        per_tile = TOKEN_TILE // state_seq
        for k, w in enumerate([256, 256, 128, 128, 128, C_ROPE]):
            if carried:
                out_specs.append(pl.BlockSpec((per_tile, None, state_seq, w), lambda i: (i, layer, 0, 0)))
            else:
                out_specs.append(pl.BlockSpec((per_tile, DEPTH, state_seq, w), lambda i: (i, 0, 0, 0)))
            out_shape.append(jax.ShapeDtypeStruct((n // state_seq, DEPTH, state_seq, w), F32))
        for k, arr in enumerate(carried):
            aliases[len(args)] = len(widths) + k
            in_specs.append(pl.BlockSpec(memory_space=pl.ANY))
            args.append(arr)
    return pl.pallas_call(
        functools.partial(_project_kernel, with_rope=with_rope, with_state=with_state, n_carried=len(carried),
                          layer=layer),
        grid=(n // TOKEN_TILE,),
        in_specs=in_specs,
        out_specs=out_specs,
        out_shape=out_shape,
        input_output_aliases=aliases,
        scratch_shapes=[pltpu.VMEM((TOKEN_TILE, D_MODEL), BF16)],
        compiler_params=_cparams(1),
        name="project_latent" if with_rope else "project_context",
    )(*args)


def _lane_mask(width, lo, size, dtype):
    lane = lax.broadcasted_iota(jnp.int32, (1, width), 1)
    return ((lane >= lo) & (lane < lo + size)).astype(dtype)


def _fold_lanes(x, op):
    r = x[:, :LANES]
    for j in range(1, x.shape[1] // LANES):
        r = op(r, x[:, j * LANES:(j + 1) * LANES])
    return r


def _diff_heads(acc, lam, sg_ref, tq):
    o_a = jnp.zeros((tq, 256), F32)
    for h in range(A_HEADS):
        o_h = acc[2 * h * tq:(2 * h + 1) * tq] - lam * acc[(2 * h + 1) * tq:(2 * h + 2) * tq]
        om = o_h * _lane_mask(256, h * A_DV, A_DV, F32)
        ms = jnp.sum(om * om, axis=-1, keepdims=True) * (1.0 / A_DV)
        o_a = o_a + om * lax.rsqrt(ms + RMS_EPS)
    return o_a * sg_ref[...]


def _finish(o_parts, x_ref, gt_ref, wo_ref, g_ref, b_ref, o_ref):
    o = jnp.concatenate(o_parts, axis=1).astype(BF16)
    f = _dot(o, wo_ref[...])
    y = DN_ALPHA * x_ref[...] + gt_ref[...] * f
    o_ref[...] = _layernorm(y, g_ref[...], b_ref[...])


def _diff_lambda(lp_ref, lam_init):
    lp = lp_ref[...]
    s1 = jnp.sum(lp[0:1, :] * lp[1:2, :], axis=-1, keepdims=True)
    s2 = jnp.sum(lp[2:3, :] * lp[3:4, :], axis=-1, keepdims=True)
    return jnp.exp(s1) - jnp.exp(s2) + lam_init


def _diff_head_t(blk, lam, tq):
    o_h = blk[:, :tq] - lam * blk[:, tq:]
    ms = jnp.mean(o_h * o_h, axis=0, keepdims=True)
    return o_h * lax.rsqrt(ms + RMS_EPS)


def _group_heads_t(o_t, tq):
    return jnp.concatenate([o_t[B_DH * g:B_DH * (g + 1), g * tq:(g + 1) * tq] for g in range(B_GROUP)], axis=0)


def _sink_row(sink_ref, layer, kv_head, tq):
    return jnp.concatenate([jnp.full((1, tq), sink_ref[layer, kv_head * B_GROUP + g] * LOG2E, F32)
                            for g in range(B_GROUP)], axis=1)


def _finish_t(parts, x_ref, gt_ref, wo_ref, g_ref, b_ref, o_ref):
    _finish([jnp.concatenate(parts, axis=0).T], x_ref, gt_ref, wo_ref, g_ref, b_ref, o_ref)


def _attn_ctx_kernel(qa_ref, ka_ref, va_ref, qb_ref, kb_ref, vb_ref, qc_ref, kc_ref,
                     x_ref, gt_ref, wo_ref, g_ref, b_ref, lp_ref, sg_ref, sink_ref, wuvt_ref,
                     o_ref, *, lam_init, layer):
    tq = qa_ref.shape[0]
    lam = _diff_lambda(lp_ref, lam_init)

    def softmax_t(s_t, sink=None):
        m = jnp.max(s_t, axis=0, keepdims=True)
        if sink is not None:
            m = jnp.maximum(m, sink)
        e = jnp.exp2(s_t - m)
        l = jnp.sum(e, axis=0, keepdims=True)
        if sink is not None:
            l = l + jnp.exp2(sink - m)
        return e.astype(BF16), l

    def values_t(v, e, l):
        return _dot(v.T, e) / l

    qa = qa_ref[...]
    q8 = jnp.concatenate([qa * _lane_mask(256, hj * A_DQK, A_DQK, BF16) for hj in range(2 * A_HEADS)], axis=0)
    e, l = softmax_t(_dot_nt(ka_ref[...], q8))
    o_t = values_t(va_ref[...], e, l)
    heads = [_diff_head_t(o_t[A_DV * h:A_DV * (h + 1), 2 * h * tq:(2 * h + 2) * tq], lam, tq)
             for h in range(A_HEADS)]
    parts = [jnp.concatenate(heads, axis=0) * sg_ref[...]]

    qb = qb_ref[...]
    kb = kb_ref[...]
    vb = vb_ref[...]
    for c in range(B_KV_HEADS):
        cs = slice(256 * c, 256 * c + 256)
        q4 = jnp.concatenate([qb[:, cs] * _lane_mask(256, g * B_DH, B_DH, BF16) for g in range(B_GROUP)], axis=0)
        e, den = softmax_t(_dot_nt(kb[:, cs], q4), _sink_row(sink_ref, layer, c, tq))
        parts.append(_group_heads_t(values_t(vb[:, cs], e, den), tq))

    qc = qc_ref[...]
    kc = kc_ref[...]
    q4 = jnp.concatenate([qc[:, 256 * h:256 * h + 256] for h in range(C_HEADS)], axis=0)
    e, l = softmax_t(_dot_nt(kc, q4))
    o_t = values_t(kc, e, l)
    o_lat_t = jnp.concatenate([o_t[:C_KV_RANK, h * tq:(h + 1) * tq] for h in range(C_HEADS)], axis=0)
    parts.append(_dot(wuvt_ref[...], o_lat_t.astype(BF16)))
    _finish_t(parts, x_ref, gt_ref, wo_ref, g_ref, b_ref, o_ref)


def _attn_ctx(proj, x, mod, layer, wo, ln_g, ln_b, lp, sg, sink, wuv, lam_init, seq):
    qa, ka, va, qb, kb, vb, qc, kc = proj
    n = x.shape[0]

    def rows(width):
        return pl.BlockSpec((seq, width), lambda i: (i, 0))

    return pl.pallas_call(
        functools.partial(_attn_ctx_kernel, lam_init=lam_init, layer=layer),
        grid=(n // seq,),
        in_specs=[rows(256), rows(256), rows(256), rows(512), rows(512), rows(512), rows(1024), rows(256),
                  rows(D_MODEL), _mod_spec(layer, 5, lambda i: 0),
                  _pick(wo, layer), _pick(ln_g, layer, 1), _pick(ln_b, layer, 1), _pick(lp, layer),
                  _pick(sg, layer), pl.BlockSpec(memory_space=pltpu.SMEM), _pick(wuv, layer)],
        out_specs=rows(D_MODEL),
        out_shape=jax.ShapeDtypeStruct((n, D_MODEL), F32),
        compiler_params=_cparams(1),
        name="attention_context",
    )(qa, ka, va, qb, kb, vb, qc, kc, x, mod, wo, ln_g, ln_b, lp, sg, sink, wuv)


def _attn_lat_kernel(qa_ref, qb_ref, qc_ref,
                     ka_ref, va_ref, kb_ref, vb_ref, kc_ref,
                     xka_ref, xva_ref, xkb_ref, xvb_ref, xkc_ref,
                     x_ref, gt_ref, wo_ref, g_ref, b_ref, lp_ref, sg_ref, sink_ref, wuv_ref,
                     o_ref, qa_scr, qc_scr, sa_scr, sc_scr, *, lam_init, layer, n_lat):
    tq = qa_ref.shape[0]
    n_ctx = xka_ref.shape[0]
    n_chunks = n_lat // KEY_CHUNK
    lam = _diff_lambda(lp_ref, lam_init)

    def key_span(c):
        return slice(n_ctx + c * KEY_CHUNK, n_ctx + (c + 1) * KEY_CHUNK)

    def key_rows(c):
        return slice(c * KEY_CHUNK, (c + 1) * KEY_CHUNK)

    def score_pass(q_scr, s_scr, k_ref, xk_ref):
        s = _dot_nt(q_scr[...], xk_ref[...])
        s_scr[:, :n_ctx] = s
        m_run = _fold_lanes(s, jnp.maximum)
        for c in range(n_chunks):
            s = _dot_nt(q_scr[...], k_ref[key_rows(c), :])
            s_scr[:, key_span(c)] = s
            m_run = jnp.maximum(m_run, _fold_lanes(s, jnp.maximum))
        return jnp.max(m_run, axis=-1, keepdims=True)

    def value_pass(s_scr, m, v_ref, xv_ref):
        e = jnp.exp2(s_scr[:, :n_ctx] - m)
        l_run = _fold_lanes(e, jnp.add)
        acc = _dot(e.astype(BF16), xv_ref[...])
        for c in range(n_chunks):
            e = jnp.exp2(s_scr[:, key_span(c)] - m)
            l_run = l_run + _fold_lanes(e, jnp.add)
            acc = acc + _dot(e.astype(BF16), v_ref[key_rows(c), :])
        return acc / jnp.sum(l_run, axis=-1, keepdims=True)

    qa = qa_ref[...]
    for hj in range(2 * A_HEADS):
        qa_scr[hj * tq:(hj + 1) * tq, :] = qa * _lane_mask(256, hj * A_DQK, A_DQK, BF16)
    for h in range(C_HEADS):
        qc_scr[h * tq:(h + 1) * tq, :] = qc_ref[:, 256 * h:256 * (h + 1)]
    m_a = score_pass(qa_scr, sa_scr, ka_ref, xka_ref)
    m_c = score_pass(qc_scr, sc_scr, kc_ref, xkc_ref)

    o_a = _diff_heads(value_pass(sa_scr, m_a, va_ref, xva_ref), lam, sg_ref, tq)

    i = pl.program_id(1)
    q0 = i * tq
    win = tq + 2 * WINDOW
    start = pl.multiple_of(jnp.clip(q0 - WINDOW, 0, n_lat - win), BLOCK)
    kpos = start + lax.broadcasted_iota(jnp.int32, (1, win), 1)
    qpos = q0 + (lax.broadcasted_iota(jnp.int32, (B_GROUP * tq, 1), 0) & (tq - 1))
    valid = jnp.abs(kpos - qpos) <= WINDOW
    qb = qb_ref[...]
    o_b = []
    for c in range(B_KV_HEADS):
        cs = slice(256 * c, 256 * c + 256)
        masks = [_lane_mask(256, g * B_DH, B_DH, BF16) for g in range(B_GROUP)]
        q4 = jnp.concatenate([qb[:, cs] * mk for mk in masks], axis=0)
        sink = jnp.concatenate([jnp.full((tq, 1), sink_ref[layer, c * B_GROUP + g] * LOG2E, F32)
                                for g in range(B_GROUP)], axis=0)
        s_ctx = _dot_nt(q4, xkb_ref[:, cs])
        s_loc = jnp.where(valid, _dot_nt(q4, kb_ref[pl.ds(start, win), cs]), NEG_BIG)
        m_fold = jnp.maximum(_fold_lanes(s_ctx, jnp.maximum), _fold_lanes(s_loc, jnp.maximum))
        m = jnp.maximum(jnp.max(m_fold, axis=-1, keepdims=True), sink)
        e_ctx = jnp.exp2(s_ctx - m)
        e_loc = jnp.exp2(s_loc - m)
        l_fold = _fold_lanes(e_ctx, jnp.add) + _fold_lanes(e_loc, jnp.add)
        den = jnp.sum(l_fold, axis=-1, keepdims=True) + jnp.exp2(sink - m)
        acc = (_dot(e_ctx.astype(BF16), xvb_ref[:, cs])
               + _dot(e_loc.astype(BF16), vb_ref[pl.ds(start, win), cs])) / den
        oc = acc[:tq] * masks[0].astype(F32)
        for g in range(1, B_GROUP):
            oc = oc + acc[g * tq:(g + 1) * tq] * masks[g].astype(F32)
        o_b.append(oc)

    acc = value_pass(sc_scr, m_c, kc_ref, xkc_ref)
    o_lat = jnp.concatenate([acc[h * tq:(h + 1) * tq, :C_KV_RANK] for h in range(C_HEADS)], axis=1)
    o_c = _dot(o_lat.astype(BF16), wuv_ref[...])

    _finish([o_a] + o_b + [o_c], x_ref, gt_ref, wo_ref, g_ref, b_ref, o_ref)


def _attn_lat(proj, ctx, x, mod, layer, wo, ln_g, ln_b, lp, sg, sink, wuv, lam_init, n_batch, n_lat):
    qa, ka, va, qb, kb, vb, qc, kc = proj
    xka, xva, xkb, xvb, xkc = ctx
    tq = BLOCK
    tiles = n_lat // tq
    n_ctx = xka.shape[2]

    def qrows(width):
        return pl.BlockSpec((tq, width), lambda bi, i: (bi * tiles + i, 0))

    def seq(width):
        return pl.BlockSpec((n_lat, width), lambda bi, i: (bi, 0), pipeline_mode=pl.Buffered(1))

    def cseq(width):
        return pl.BlockSpec((None, None, n_ctx, width), lambda bi, i: (bi, layer, 0, 0))

    return pl.pallas_call(
        functools.partial(_attn_lat_kernel, lam_init=lam_init, layer=layer, n_lat=n_lat),
        grid=(n_batch, tiles),
        in_specs=[qrows(256), qrows(512), qrows(1024),
                  seq(256), seq(256), seq(512), seq(512), seq(256),
                  cseq(256), cseq(256), cseq(512), cseq(512), cseq(256),
                  qrows(D_MODEL),
                  pl.BlockSpec((None, None, None, 1, D_MODEL), lambda bi, i: (layer, 1 + bi, 5, 0, 0)),
                  _pick(wo, layer), _pick(ln_g, layer, 1), _pick(ln_b, layer, 1), _pick(lp, layer),
                  _pick(sg, layer), pl.BlockSpec(memory_space=pltpu.SMEM), _pick(wuv, layer)],
        out_specs=qrows(D_MODEL),
        out_shape=jax.ShapeDtypeStruct((n_batch * n_lat, D_MODEL), F32),
        scratch_shapes=[pltpu.VMEM((2 * A_HEADS * tq, 256), BF16),
                        pltpu.VMEM((C_HEADS * tq, 256), BF16),
                        pltpu.VMEM((2 * A_HEADS * tq, n_ctx + n_lat), F32),
                        pltpu.VMEM((C_HEADS * tq, n_ctx + n_lat), F32)],
        compiler_params=_cparams(2),
        name="attention_latent",
    )(qa, qb, qc, ka, va, kb, vb, kc, xka, xva, xkb, xvb, xkc, x, mod, wo, ln_g, ln_b, lp, sg, sink, wuv)


def _rope_tables(rows, dim):
    row = jnp.repeat(jnp.arange(rows, dtype=F32), GRID_W)
    col = jnp.tile(jnp.arange(GRID_W, dtype=F32), rows)
    a = dim // 2
    inv = jnp.power(ROPE_BASE, -jnp.arange(0, a, 2, dtype=F32) / a)
    ar = row[:, None] * inv[None, :]
    ac = col[:, None] * inv[None, :]
    ang = jnp.concatenate([ar, ar, ac, ac], axis=-1)
    reps = LANES // dim
    cos = jnp.tile(jnp.cos(ang), (1, reps))
    sin = jnp.tile(jnp.sin(ang), (1, reps))
    lower = (np.arange(LANES) % (dim // 2)) < (dim // 4)
    sin_lo = jnp.where(lower[None, :], -sin, 0.0)
    sin_hi = jnp.where(lower[None, :], 0.0, sin)
    return cos, sin_lo, sin_hi


def _c_weights(w_q_up, w_kv_up):
    n_l = w_q_up.shape[0]
    wq = w_q_up.reshape(n_l, C_Q_RANK, C_HEADS, C_NOPE + C_ROPE)
    wqu = jnp.concatenate([wq[..., :C_NOPE].reshape(n_l, C_Q_RANK, -1),
                           wq[..., C_NOPE:].reshape(n_l, C_Q_RANK, -1)], axis=-1)
    eye_h = np.eye(C_HEADS, dtype=np.float32)
    w_uk = w_kv_up[..., :C_NOPE]
    w_uv = w_kv_up[..., C_NOPE:]
    top = jnp.einsum('lchd,hg->lhdgc', w_uk, eye_h)
    top = jnp.pad(top, ((0, 0),) * 4 + ((0, 256 - C_KV_RANK),)).reshape(n_l, C_HEADS * C_NOPE, C_HEADS * 256)
    place = np.zeros((C_HEADS, C_ROPE, C_HEADS, 256), np.float32)
    for h in range(C_HEADS):
        place[h, np.arange(C_ROPE), h, C_KV_RANK + np.arange(C_ROPE)] = 1.0
    bottom = jnp.broadcast_to(place.reshape(1, C_HEADS * C_ROPE, C_HEADS * 256), (n_l, C_HEADS * C_ROPE, C_HEADS * 256))
    wc2 = jnp.concatenate([top, bottom], axis=1)
    wuv = jnp.einsum('lchd,hg->lhcgd', w_uv, eye_h).reshape(n_l, C_HEADS * C_KV_RANK, C_HEADS * C_DV)
    return wqu.astype(BF16), wc2.astype(BF16), wuv.astype(BF16)


def _rep_kv(t):
    lead = t.shape[:3]
    return jnp.broadcast_to(t[..., None, :], lead + (B_KV_HEADS, B_GROUP, B_DH)).reshape(lead + (-1,))


def kernel(x_prompt, x_sample, cache_a_k, cache_a_v, cache_b_k, cache_b_v, cache_c_kv, cache_c_kpe, c, c_ctx,
           w_mod, b_mod, ln_g, ln_b, ffn_w1, ffn_w3, ffn_w2, w_in, w_o, a_lambda, a_subln_g, b_sink,
           c_q_norm_g, c_w_q_up, c_kv_norm_g, c_w_kv_up):
    n_req, seq, _ = x_prompt.shape
    n_dec, n_lat, _ = x_sample.shape
    n_ctx = cache_a_k.shape[2]
    assert n_lat % BLOCK == 0 and n_lat % TOKEN_TILE == 0 and (n_req * seq) % TOKEN_TILE == 0
    assert 1 + n_dec <= MOD_ROWS

    cond8 = jnp.concatenate([c_ctx[None, :], c, jnp.zeros((MOD_ROWS - 1 - n_dec, D_MODEL), F32)], axis=0)
    mod = _modulation(cond8, w_mod, b_mod[:, None, :]).reshape(DEPTH, MOD_ROWS, N_MOD, 1, D_MODEL)

    w1 = ffn_w1.astype(BF16)
    w3 = ffn_w3.astype(BF16)
    w2 = ffn_w2.astype(BF16)
    w_in_b = w_in.astype(BF16)
    w_ck = jnp.pad(w_in[:, :, OFF_CKV:], ((0, 0), (0, 0), (0, 256 - C_KV_RANK - C_ROPE))).astype(BF16)
    wo = w_o.astype(BF16)
    g4 = ln_g.reshape(DEPTH, 3, 1, D_MODEL)
    b4 = ln_b.reshape(DEPTH, 3, 1, D_MODEL)
    qg = c_q_norm_g.reshape(DEPTH, 1, C_Q_RANK)
    kvg = c_kv_norm_g.reshape(DEPTH, 1, C_KV_RANK)
    wqu, wc2, wuv = _c_weights(c_w_q_up, c_w_kv_up)
    lam_inits = [0.8 - 0.6 * math.exp(-0.3 * l) for l in range(DEPTH)]
    sg = (jnp.tile(a_subln_g, (1, A_HEADS)) * (1.0 - np.asarray(lam_inits, np.float32))[:, None])[:, None, :]
    sg_t = jnp.broadcast_to(sg[:, 0, :, None], (DEPTH, A_HEADS * A_DV, seq))
    wuv_t = jnp.swapaxes(wuv, 1, 2)
    ctx_pad = jnp.zeros(cache_c_kv.shape[:3] + (256 - C_KV_RANK - C_ROPE,), F32)
    ctx = (cache_a_k.reshape(n_dec, DEPTH, n_ctx, -1), cache_a_v.reshape(n_dec, DEPTH, n_ctx, -1),
           _rep_kv(cache_b_k), _rep_kv(cache_b_v),
           jnp.concatenate([cache_c_kv, cache_c_kpe, ctx_pad], axis=-1))
    ctx = tuple(t.astype(BF16) for t in ctx)
    tabs = _rope_tables(n_lat // GRID_W, A_DQK) + _rope_tables(n_lat // GRID_W, B_DH)
    tiles_per_lat = n_lat // TOKEN_TILE

    def prompt_row(i):
        return 0

    def sample_row(i):
        return 1 + i // tiles_per_lat

    xp = x_prompt.reshape(n_req * seq, D_MODEL)
    xs = x_sample.reshape(n_dec * n_lat, D_MODEL)
    states = ()
    for l in range(DEPTH):
        proj_w = (w_in_b, w_ck, qg, wqu, wc2, kvg)
        attn_w = (wo, g4, b4, a_lambda, sg, b_sink, wuv, lam_inits[l])
        attn_w_ctx = (wo, g4, b4, a_lambda, sg_t, b_sink, wuv_t, lam_inits[l])

        xp = _ffn(xp, mod, l, 0, w1, w3, w2, g4, b4, prompt_row)
        outs = _project(xp, mod, l, *proj_w, prompt_row, state_seq=seq, carried=states)
        states = tuple(outs[8:])
        xp = _attn_ctx(outs[:8], xp, mod, l, *attn_w_ctx, seq)
        xp = _ffn(xp, mod, l, 1, w1, w3, w2, g4, b4, prompt_row)

        xs = _ffn(xs, mod, l, 0, w1, w3, w2, g4, b4, sample_row)
        outs = _project(xs, mod, l, *proj_w, sample_row, tabs=tabs)
        xs = _attn_lat(outs, ctx, xs, mod, l, *attn_w, n_dec, n_lat)
        xs = _ffn(xs, mod, l, 1, w1, w3, w2, g4, b4, sample_row)

    def state(k, tail):
        return states[k].reshape((n_req, DEPTH, seq) + tail)

    return (xp.reshape(n_req, seq, D_MODEL), xs.reshape(n_dec, n_lat, D_MODEL),
            state(0, (A_HEADS, 2 * A_DQK)), state(1, (A_HEADS, A_DV)),
            state(2, (B_KV_HEADS, B_DH)), state(3, (B_KV_HEADS, B_DH)),
            state(4, (C_KV_RANK,)), state(5, (C_ROPE,)))
```
